```python
import math
import jax
import jax.numpy as jnp
from jax import lax
import numpy as np

D_MODEL = 2048
BATCH = 2
SEQ = 16384
DEPTH = 4

GRID_W = 64
CTX_LEN = 256

GROUP_WIDTH = D_MODEL // 4
HEAD_DIM = 128

DN_HEADS = GROUP_WIDTH // HEAD_DIM
DN_CONV = 4
DN_CHUNK = 64
LRU_WIDTH = GROUP_WIDTH
LRU_BLOCKS = GROUP_WIDTH // HEAD_DIM
LRU_CONV = 4
LRU_C = 8.0
DA_HEADS = GROUP_WIDTH // HEAD_DIM
DA_QK = HEAD_DIM // 2
DA_V = HEAD_DIM
Q_BLOCK = 128
ROPE_BASE = 10000.0
SC_WIDTH = GROUP_WIDTH
SC_CONV = 3
N_EXPERTS = 64
TOP_K = 6
EXPERT_FF = 256
SHARED_FF = 512
ROUTED_SCALE = 2.5
MOE_BLOCK = 256

DN_IN = 4 * GROUP_WIDTH + 4 * DN_HEADS
LRU_IN = 2 * LRU_WIDTH
DA_IN = 3 * GROUP_WIDTH
SC_IN = 3 * SC_WIDTH
IN_WIDTH = DN_IN + LRU_IN + DA_IN + SC_IN

DEEPNORM_ALPHA = (2 * DEPTH) ** 0.25
DEEPNORM_BETA = (8 * DEPTH) ** -0.25

kernel_name = 'hybrid_parallel_groups_moe_diffusion_trunk'


def layer_norm(t, g, b, eps=1e-5):
    tf = t.astype(jnp.float32)
    mu = jnp.mean(tf, -1, keepdims=True)
    var = jnp.mean(jnp.square(tf - mu), -1, keepdims=True)
    return ((tf - mu) * lax.rsqrt(var + eps) * g + b).astype(t.dtype)


def rms_norm(t, g, eps=1e-6):
    tf = t.astype(jnp.float32)
    return (tf * lax.rsqrt(jnp.mean(jnp.square(tf), -1, keepdims=True) + eps) * g).astype(t.dtype)


def l2norm(t, eps=1e-6):
    return t * lax.rsqrt(jnp.sum(jnp.square(t), -1, keepdims=True) + eps)


def modulate(t, shift, scale):
    return t * (1 + scale) + shift


def depthwise_conv(t, w, pad_left, pad_right):
    return lax.conv_general_dilated(t, w.T[:, None, :].astype(t.dtype), window_strides=(1,),
                                    padding=[(pad_left, pad_right)],
                                    dimension_numbers=('NWC', 'WIO', 'NWC'),
                                    feature_group_count=t.shape[-1])


def axial_rope(n_lat, dim):
    rows = n_lat // GRID_W
    n_freq = dim // 4
    inv = ROPE_BASE ** (-jnp.arange(n_freq, dtype=jnp.float32) / n_freq)
    row_pos = jnp.repeat(jnp.arange(rows, dtype=jnp.float32), GRID_W)
    col_pos = (jnp.arange(n_lat) % GRID_W).astype(jnp.float32)
    ang = jnp.concatenate([row_pos[:, None] * inv, col_pos[:, None] * inv], -1)
    return jnp.cos(ang), jnp.sin(ang)


def apply_rope(t, cos, sin):
    tf = t.astype(jnp.float32).reshape(*t.shape[:-1], -1, 2)
    t1, t2 = tf[..., 0], tf[..., 1]
    out = jnp.stack([t1 * cos - t2 * sin, t1 * sin + t2 * cos], -1)
    return out.reshape(t.shape).astype(t.dtype)


def gdn_chunked(q, k, v, g, beta, s0, with_out):
    B, H, n, _ = q.shape
    C = DN_CHUNK
    nc = n // C
    blk = lambda t: t.reshape(B, H, nc, C, *t.shape[3:])
    q, k, v, g, beta = blk(q), blk(k), blk(v), blk(g), blk(beta)
    g_cum = jnp.cumsum(g, -1)
    incl = jnp.tril(jnp.ones((C, C), dtype=bool))
    decay = jnp.exp(jnp.where(incl, g_cum[..., :, None] - g_cum[..., None, :], -jnp.inf))
    k_beta = k * beta[..., None]
    m = jnp.tril(jnp.einsum('bhcid,bhcjd->bhcij', k_beta, k) * decay, -1)
    eye = jnp.eye(C, dtype=m.dtype)
    t_inv = lax.linalg.triangular_solve(eye + m, jnp.broadcast_to(eye, m.shape), left_side=True, lower=True)
    u = t_inv @ (v * beta[..., None])
    w = t_inv @ (k_beta * jnp.exp(g_cum)[..., None])
    g_last = g_cum[..., -1]
    k_dec = k * jnp.exp(g_last[..., None] - g_cum)[..., None]
    front = lambda t: jnp.moveaxis(t, 2, 0)

    def advance(s, w_c, u_c, k_c, gl_c):
        v_new = u_c - w_c @ s
        s_next = s * jnp.exp(gl_c)[..., None, None] + jnp.swapaxes(k_c, -1, -2) @ v_new
        return s_next, v_new

    if not with_out:
        def step_state(s, inp):
            s_next, _ = advance(s, *inp)
            return s_next, None
        s_fin, _ = lax.scan(step_state, s0, tuple(map(front, (w, u, k_dec, g_last))))
        return s_fin, None

    attn = jnp.einsum('bhcid,bhcjd->bhcij', q, k) * decay
    q_dec = q * jnp.exp(g_cum)[..., None]

    def step(s, inp):
        w_c, u_c, k_c, gl_c, q_c, a_c = inp
        s_next, v_new = advance(s, w_c, u_c, k_c, gl_c)
        return s_next, q_c @ s + a_c @ v_new

    s_fin, o = lax.scan(step, s0, tuple(map(front, (w, u, k_dec, g_last, q_dec, attn))))
    return s_fin, jnp.moveaxis(o, 0, 2).reshape(B, H, n, -1)


def gdn_mixer(p_ctx, p_lat, conv_w, a_log, dt_bias, norm_g, with_ctx):
    G = GROUP_WIDTH
    f32 = jnp.float32

    def prep(p):
        B, n, _ = p.shape
        qkv = jax.nn.silu(depthwise_conv(p[..., :3 * G], conv_w, 2, 1)).astype(f32)
        heads = lambda t: t.reshape(B, n, DN_HEADS, HEAD_DIM).transpose(0, 2, 1, 3)
        q = l2norm(heads(qkv[..., :G])) * HEAD_DIM ** -0.5
        k = l2norm(heads(qkv[..., G:2 * G]))
        v = heads(qkv[..., 2 * G:])
        gates = p[..., 4 * G:].astype(f32).reshape(B, n, 2, 2, DN_HEADS)
        g = -jnp.exp(a_log) * jax.nn.softplus(gates[..., 0, :] + dt_bias)
        beta = jax.nn.sigmoid(gates[..., 1, :])
        return q, k, v, g.transpose(2, 0, 3, 1), beta.transpose(2, 0, 3, 1)

    qc, kc, vc, gc, bc = prep(p_ctx)
    ql, kl, vl, gl, bl = prep(p_lat)
    flip = lambda t: jnp.flip(t, 2)
    s0 = jnp.zeros(qc.shape[:2] + (HEAD_DIM, HEAD_DIM), f32)
    s_fwd, o_c_f = gdn_chunked(qc, kc, vc, gc[0], bc[0], s0, with_ctx)
    s_bwd, o_c_b = gdn_chunked(flip(qc), flip(kc), flip(vc), flip(gc[1]), flip(bc[1]), s0, with_ctx)
    _, o_l_f = gdn_chunked(ql, kl, vl, gl[0], bl[0], s_fwd, True)
    _, o_l_b = gdn_chunked(flip(ql), flip(kl), flip(vl), flip(gl[1]), flip(bl[1]), s_bwd, True)

    def finish(o, p):
        B, _, n, _ = o.shape
        o = rms_norm(o, norm_g).transpose(0, 2, 1, 3).reshape(B, n, G)
        return (o * jax.nn.silu(p[..., 3 * G:4 * G].astype(f32))).astype(p.dtype)

    out_l = finish(o_l_f + flip(o_l_b), p_lat)
    out_c = finish(o_c_f + flip(o_c_b), p_ctx) if with_ctx else None
    return out_c, out_l


def linear_scan(log_a, b, h0):
    def combine(e1, e2):
        return e1[0] + e2[0], e1[1] * jnp.exp(e2[0]) + e2[1]
    cum_log_a, h = lax.associative_scan(combine, (log_a, b), axis=1)
    return h + jnp.exp(cum_log_a) * h0[:, None, :]


def rglru_mixer(p_ctx, p_lat, conv_w, conv_b, wa, ba, wx, bx, lam, with_ctx):
    W = LRU_WIDTH
    f32 = jnp.float32
    xc = (depthwise_conv(p_ctx[..., :W], conv_w, 2, 1) + conv_b).astype(f32)
    xl = (depthwise_conv(p_lat[..., :W], conv_w, 2, 1) + conv_b).astype(f32)

    def gates(xb, d):
        B, n, _ = xb.shape
        xh = xb.reshape(B, n, LRU_BLOCKS, W // LRU_BLOCKS)
        r = jax.nn.sigmoid(jnp.einsum('bnhi,hij->bnhj', xh, wa[d].astype(f32)).reshape(B, n, W) + ba[d])
        i = jax.nn.sigmoid(jnp.einsum('bnhi,hij->bnhj', xh, wx[d].astype(f32)).reshape(B, n, W) + bx[d])
        log_a = -LRU_C * r * jax.nn.softplus(-lam[d])
        return log_a, jnp.sqrt(-jnp.expm1(2.0 * log_a)) * (i * xb)

    flip = lambda t: jnp.flip(t, 1)
    h0 = jnp.zeros((xc.shape[0], W), f32)
    la, bb = gates(xc, 0)
    hc_f = linear_scan(la, bb, h0)
    la, bb = gates(xc, 1)
    hc_b = flip(linear_scan(flip(la), flip(bb), h0))
    la, bb = gates(xl, 0)
    hl_f = linear_scan(la, bb, hc_f[:, -1])
    la, bb = gates(xl, 1)
    hl_b = flip(linear_scan(flip(la), flip(bb), hc_b[:, 0]))
    out_l = ((hl_f + hl_b) * jax.nn.gelu(p_lat[..., W:].astype(f32))).astype(p_lat.dtype)
    out_c = ((hc_f + hc_b) * jax.nn.gelu(p_ctx[..., W:].astype(f32))).astype(p_ctx.dtype) if with_ctx else None
    return out_c, out_l


def block_diff_attention(q, k, v, lam):
    B, n = q.shape[:2]
    nb = n // Q_BLOCK
    qb = jnp.moveaxis(q.reshape(B, nb, Q_BLOCK, *q.shape[2:]), 1, 0)

    def one(q_blk):
        s = jnp.einsum('bqhmd,bkhmd->bhmqk', q_blk, k, preferred_element_type=jnp.float32)
        p = jax.nn.softmax(s, axis=-1)
        a = p[:, :, 0] - lam * p[:, :, 1]
        return jnp.einsum('bhqk,bkhd->bqhd', a.astype(v.dtype), v)

    o = lax.map(one, qb)
    return jnp.moveaxis(o, 0, 1).reshape(B, n, *o.shape[3:])


def diff_attn_mixer(p_ctx, p_lat, rope_cos, rope_sin, lam_vecs, norm_g, layer_idx, with_ctx):
    G = GROUP_WIDTH

    def split_heads(p):
        B, n, _ = p.shape
        q = p[..., :G].reshape(B, n, DA_HEADS, 2, DA_QK)
        k = p[..., G:2 * G].reshape(B, n, DA_HEADS, 2, DA_QK)
        v = p[..., 2 * G:].reshape(B, n, DA_HEADS, DA_V)
        return q, k, v

    qc, kc, vc = split_heads(p_ctx)
    ql, kl, vl = split_heads(p_lat)
    cos = rope_cos[None, :, None, None, :]
    sin = rope_sin[None, :, None, None, :]
    scale = DA_QK ** -0.5
    ql = apply_rope(ql, cos, sin) * scale
    kl = apply_rope(kl, cos, sin)
    lam_init = 0.8 - 0.6 * math.exp(-0.3 * layer_idx)
    lv = lam_vecs.astype(jnp.float32)
    lam = jnp.exp(jnp.sum(lv[0] * lv[1])) - jnp.exp(jnp.sum(lv[2] * lv[3])) + lam_init

    def finish(o):
        B, n = o.shape[:2]
        return (rms_norm(o, norm_g) * (1.0 - lam_init)).reshape(B, n, G)

    out_l = finish(block_diff_attention(ql, jnp.concatenate([kc, kl], 1), jnp.concatenate([vc, vl], 1), lam))
    out_c = finish(block_diff_attention(qc * scale, kc, vc, lam)) if with_ctx else None
    return out_c, out_l


def short_conv_mixer(p, conv_w):
    b_g, c_g, xs = jnp.split(p, 3, -1)
    return b_g * depthwise_conv(c_g * xs, conv_w, 1, 1)


def moe_ffn(tok, router_w, router_bias, w1, w3, w2, sw1, sw3, sw2):
    T, D = tok.shape
    scores = jax.nn.sigmoid(jnp.matmul(tok, router_w, preferred_element_type=jnp.float32))
    _, idx = lax.top_k(scores + router_bias.astype(jnp.float32), TOP_K)
    wts = jnp.take_along_axis(scores, idx, -1)
    wts = wts / jnp.sum(wts, -1, keepdims=True) * ROUTED_SCALE
    flat_e = idx.reshape(-1)
    flat_tok = jnp.repeat(jnp.arange(T, dtype=jnp.int32), TOP_K)
    order = jnp.argsort(flat_e)
    e_sorted = flat_e[order]
    counts = jnp.bincount(flat_e, length=N_EXPERTS)
    starts = jnp.cumsum(counts) - counts
    padded = (counts + MOE_BLOCK - 1) // MOE_BLOCK * MOE_BLOCK
    pad_ends = jnp.cumsum(padded)
    dest = pad_ends[e_sorted] - padded[e_sorted] + jnp.arange(T * TOP_K) - starts[e_sorted]
    n_blocks = -(-(T * TOP_K) // MOE_BLOCK) + N_EXPERTS
    buf_tok = jnp.zeros((n_blocks * MOE_BLOCK,), jnp.int32).at[dest].set(flat_tok[order])
    buf_w = jnp.zeros((n_blocks * MOE_BLOCK,), jnp.float32).at[dest].set(wts.reshape(-1)[order])
    blk_e = jnp.minimum(jnp.searchsorted(pad_ends, jnp.arange(n_blocks) * MOE_BLOCK, side='right'), N_EXPERTS - 1)

    def step(acc, inp):
        e, rows, wr = inp
        xb = tok[rows]
        hid = jax.nn.silu(xb @ w1[e]) * (xb @ w3[e])
        return acc.at[rows].add((hid @ w2[e]) * wr[:, None].astype(tok.dtype)), None

    routed, _ = lax.scan(step, jnp.zeros_like(tok),
                         (blk_e, buf_tok.reshape(n_blocks, MOE_BLOCK), buf_w.reshape(n_blocks, MOE_BLOCK)))
    shared = (jax.nn.silu(tok @ sw1) * (tok @ sw3)) @ sw2
    return routed + shared


def setup_inputs(seed: int = 0) -> dict:
    key = jax.random.key(seed)
    ks = jax.random.split(key, 48)
    counter = iter(range(48))
    f32 = jnp.float32
    D = D_MODEL

    def nrm(shape, scale):
        return jax.random.normal(ks[next(counter)], shape, f32) * scale

    def unif(shape, lo, hi):
        return jax.random.uniform(ks[next(counter)], shape, f32, lo, hi)

    x = nrm((BATCH, SEQ, D), 1.0)
    c = nrm((BATCH, D), 1.0)
    ctx = nrm((BATCH, CTX_LEN, D), 1.0)
    c_ctx = nrm((D,), 1.0)
    w_ada = nrm((DEPTH, D, 6 * D), 0.5 * D ** -0.5)
    b_ada = nrm((DEPTH, 6 * D), 0.02)
    w_in = nrm((DEPTH, D, IN_WIDTH), D ** -0.5)
    dn_conv = nrm((DEPTH, 3 * GROUP_WIDTH, DN_CONV), DN_CONV ** -0.5)
    dn_a_log = jnp.log(unif((DEPTH, 2, DN_HEADS), 1.0, 16.0))
    dt = jnp.exp(unif((DEPTH, 2, DN_HEADS), math.log(1e-3), math.log(1e-1)))
    dn_dt_bias = dt + jnp.log(-jnp.expm1(-dt))
    dn_norm = 1.0 + nrm((DEPTH, HEAD_DIM), 0.02)
    lru_conv = nrm((DEPTH, LRU_WIDTH, LRU_CONV), LRU_CONV ** -0.5)
    lru_conv_b = nrm((DEPTH, LRU_WIDTH), 0.02)
    bw = LRU_WIDTH // LRU_BLOCKS
    lru_wa = nrm((DEPTH, 2, LRU_BLOCKS, bw, bw), bw ** -0.5)
    lru_ba = nrm((DEPTH, 2, LRU_WIDTH), 0.02)
    lru_wx = nrm((DEPTH, 2, LRU_BLOCKS, bw, bw), bw ** -0.5)
    lru_bx = nrm((DEPTH, 2, LRU_WIDTH), 0.02)
    a_pow = unif((DEPTH, 2, LRU_WIDTH), 0.9, 0.999) ** (1.0 / LRU_C)
    lru_lambda = jnp.log(a_pow) - jnp.log1p(-a_pow)
    da_lambda = nrm((DEPTH, 4, DA_QK), 0.1)
    da_norm = 1.0 + nrm((DEPTH, DA_V), 0.02)
    sc_conv = nrm((DEPTH, SC_WIDTH, SC_CONV), SC_CONV ** -0.5)
    w_out = nrm((DEPTH, D, D), DEEPNORM_BETA * D ** -0.5)
    ln1_g = 1.0 + nrm((DEPTH, D), 0.02)
    ln1_b = nrm((DEPTH, D), 0.02)
    router_w = nrm((DEPTH, D, N_EXPERTS), D ** -0.5)
    router_bias = nrm((DEPTH, N_EXPERTS), 0.01)
    exp_w1 = nrm((DEPTH, N_EXPERTS, D, EXPERT_FF), D ** -0.5)
    exp_w3 = nrm((DEPTH, N_EXPERTS, D, EXPERT_FF), D ** -0.5)
    exp_w2 = nrm((DEPTH, N_EXPERTS, EXPERT_FF, D), DEEPNORM_BETA * EXPERT_FF ** -0.5)
    sh_w1 = nrm((DEPTH, D, SHARED_FF), D ** -0.5)
    sh_w3 = nrm((DEPTH, D, SHARED_FF), D ** -0.5)
    sh_w2 = nrm((DEPTH, SHARED_FF, D), DEEPNORM_BETA * SHARED_FF ** -0.5)
    ln2_g = 1.0 + nrm((DEPTH, D), 0.02)
    ln2_b = nrm((DEPTH, D), 0.02)
    return {'x': x, 'c': c, 'ctx': ctx, 'c_ctx': c_ctx, 'w_ada': w_ada, 'b_ada': b_ada, 'w_in': w_in,
            'dn_conv': dn_conv, 'dn_a_log': dn_a_log, 'dn_dt_bias': dn_dt_bias, 'dn_norm': dn_norm,
            'lru_conv': lru_conv, 'lru_conv_b': lru_conv_b, 'lru_wa': lru_wa, 'lru_ba': lru_ba,
            'lru_wx': lru_wx, 'lru_bx': lru_bx, 'lru_lambda': lru_lambda, 'da_lambda': da_lambda,
            'da_norm': da_norm, 'sc_conv': sc_conv, 'w_out': w_out, 'ln1_g': ln1_g, 'ln1_b': ln1_b,
            'router_w': router_w, 'router_bias': router_bias, 'exp_w1': exp_w1, 'exp_w3': exp_w3,
            'exp_w2': exp_w2, 'sh_w1': sh_w1, 'sh_w3': sh_w3, 'sh_w2': sh_w2, 'ln2_g': ln2_g, 'ln2_b': ln2_b}


def reference(x, c, ctx, c_ctx, w_ada, b_ada, w_in, dn_conv, dn_a_log, dn_dt_bias, dn_norm,
              lru_conv, lru_conv_b, lru_wa, lru_ba, lru_wx, lru_bx, lru_lambda, da_lambda, da_norm,
              sc_conv, w_out, ln1_g, ln1_b, router_w, router_bias, exp_w1, exp_w3, exp_w2,
              sh_w1, sh_w3, sh_w2, ln2_g, ln2_b):
    alpha = DEEPNORM_ALPHA
    B, N, D = x.shape
    L = ctx.shape[1]
    col_splits = [DN_IN, DN_IN + LRU_IN, DN_IN + LRU_IN + DA_IN]
    silu_c = jax.nn.silu(c)[:, None, :]
    silu_cc = jax.nn.silu(c_ctx)[None, None, :]
    rope_cos, rope_sin = axial_rope(N, DA_QK)
    xc = ctx
    for l in range(DEPTH):
        with_ctx = l < DEPTH - 1
        sh1, s1, g1, sh2, s2, g2 = jnp.split(silu_c @ w_ada[l] + b_ada[l], 6, -1)
        sh1c, s1c, g1c, sh2c, s2c, g2c = jnp.split(silu_cc @ w_ada[l] + b_ada[l], 6, -1)
        p_lat = jnp.split(modulate(x, sh1, s1) @ w_in[l], col_splits, -1)
        p_ctx = jnp.split(modulate(xc, sh1c, s1c) @ w_in[l], col_splits, -1)
        a_c, a_l = gdn_mixer(p_ctx[0], p_lat[0], dn_conv[l], dn_a_log[l], dn_dt_bias[l], dn_norm[l], with_ctx)
        b_c, b_l = rglru_mixer(p_ctx[1], p_lat[1], lru_conv[l], lru_conv_b[l], lru_wa[l], lru_ba[l],
                               lru_wx[l], lru_bx[l], lru_lambda[l], with_ctx)
        c_c, c_l = diff_attn_mixer(p_ctx[2], p_lat[2], rope_cos, rope_sin, da_lambda[l], da_norm[l], l, with_ctx)
        d_l = short_conv_mixer(p_lat[3], sc_conv[l])
        mix_l = jnp.concatenate([a_l, b_l, c_l, d_l], -1) @ w_out[l]
        x_mid = layer_norm(alpha * x + g1 * mix_l, ln1_g[l], ln1_b[l])
        moe_args = (router_w[l], router_bias[l], exp_w1[l], exp_w3[l], exp_w2[l], sh_w1[l], sh_w3[l], sh_w2[l])
        if with_ctx:
            d_c = short_conv_mixer(p_ctx[3], sc_conv[l])
            mix_c = jnp.concatenate([a_c, b_c, c_c, d_c], -1) @ w_out[l]
            xc_mid = layer_norm(alpha * xc + g1c * mix_c, ln1_g[l], ln1_b[l])
            tok = jnp.concatenate([modulate(xc_mid, sh2c, s2c).reshape(-1, D),
                                   modulate(x_mid, sh2, s2).reshape(-1, D)], 0)
            y = moe_ffn(tok, *moe_args)
            xc = layer_norm(alpha * xc_mid + g2c * y[:B * L].reshape(B, L, D), ln2_g[l], ln2_b[l])
            y_l = y[B * L:].reshape(B, N, D)
        else:
            y_l = moe_ffn(modulate(x_mid, sh2, s2).reshape(-1, D), *moe_args).reshape(B, N, D)
        x = layer_norm(alpha * x_mid + g2 * y_l, ln2_g[l], ln2_b[l])
    return x
```

```python
import functools
import math

import jax
import jax.numpy as jnp
from jax import lax
from jax.experimental import pallas as pl
from jax.experimental.pallas import tpu as pltpu

F32 = jnp.float32
BF16 = jnp.bfloat16

GRID_W = 64
GROUP_WIDTH = 512
HEAD_DIM = 128
N_HEADS = GROUP_WIDTH // HEAD_DIM
DN_CHUNK = 64
LRU_C = 8.0
DA_QK = HEAD_DIM // 2
ROPE_BASE = 10000.0
N_EXPERTS = 64
TOP_K = 6
ROUTED_SCALE = 2.5

V7X_VMEM_BYTES = 64 * 1024 * 1024
VMEM_LIMIT = V7X_VMEM_BYTES * 3 // 4


def _params(n_axes):
    return pltpu.CompilerParams(dimension_semantics=("arbitrary",) * n_axes, vmem_limit_bytes=VMEM_LIMIT)


def _pick_tile(n, candidates):
    for c in candidates:
        if n % c == 0:
            return c
    return n


def _mm_kernel(a_ref, w_ref, o_ref):
    o_ref[...] = jnp.dot(a_ref[...].astype(BF16), w_ref[...], preferred_element_type=F32).astype(o_ref.dtype)


def dense_matmul(a, w, out_dtype=F32):
    M, K = a.shape
    N = w.shape[1]
    tm = _pick_tile(M, (1024, 512, 256, 128))
    tn = _pick_tile(N, (512, 256, 128))
    return pl.pallas_call(
        _mm_kernel,
        grid=(M // tm, N // tn),
        in_specs=[pl.BlockSpec((tm, K), lambda i, j: (i, 0)),
                  pl.BlockSpec((K, tn), lambda i, j: (0, j))],
        out_specs=pl.BlockSpec((tm, tn), lambda i, j: (i, j)),
        out_shape=jax.ShapeDtypeStruct((M, N), out_dtype),
        compiler_params=_params(2),
        name="dense_matmul",
    )(a, w)


def _attn_kernel(lam_ref, g_ref, q_ref, k_ref, v_ref, o_ref, m_scr, l_scr, acc_scr, *, tk, n_chunks, out_scale):
    q = q_ref[0]
    lane = lax.broadcasted_iota(jnp.int32, q.shape, 1)
    zero = jnp.zeros_like(q)
    q_maps = (jnp.where(lane < DA_QK, q, zero), jnp.where(lane >= DA_QK, q, zero))
    m_scr[...] = jnp.full(m_scr.shape, -jnp.inf, F32)
    l_scr[...] = jnp.zeros(l_scr.shape, F32)
    acc_scr[...] = jnp.zeros(acc_scr.shape, F32)

    def chunk(c, carry):
        off = pl.multiple_of(c * tk, tk)
        kc = k_ref[0, pl.ds(off, tk), :]
        vc = v_ref[0, pl.ds(off, tk), :]
        for mp in range(2):
            s = lax.dot_general(q_maps[mp], kc, (((1,), (1,)), ((), ())), preferred_element_type=F32)
            m_prev = m_scr[mp]
            m_new = jnp.maximum(m_prev, jnp.max(s, axis=-1, keepdims=True))
            p = jnp.exp(s - m_new)
            alpha = jnp.exp(m_prev - m_new)
            l_scr[mp] = alpha * l_scr[mp] + jnp.sum(p, axis=-1, keepdims=True)
            acc_scr[mp] = alpha * acc_scr[mp] + jnp.dot(p.astype(BF16), vc, preferred_element_type=F32)
            m_scr[mp] = m_new
        return carry

    lax.fori_loop(0, n_chunks, chunk, 0)
    o = acc_scr[0] / l_scr[0] - lam_ref[...] * (acc_scr[1] / l_scr[1])
    o = o * lax.rsqrt(jnp.mean(o * o, axis=-1, keepdims=True) + 1e-6)
    o_ref[0] = o * g_ref[...] * out_scale


def diff_attention(q, k, v, lam, norm_g, out_scale):
    B, n, G = q.shape
    m = k.shape[1]
    tq = _pick_tile(n, (512, 256, 128))
    tk = _pick_tile(m, (640, 512, 256, 128))
    kern = functools.partial(_attn_kernel, tk=tk, n_chunks=m // tk, out_scale=out_scale)
    lam_row = jnp.full((1, HEAD_DIM), lam, F32)
    return pl.pallas_call(
        kern,
        grid=(B, N_HEADS, n // tq),
        in_specs=[pl.BlockSpec((1, HEAD_DIM), lambda b, h, i: (0, 0)),
                  pl.BlockSpec((1, HEAD_DIM), lambda b, h, i: (0, 0)),
                  pl.BlockSpec((1, tq, HEAD_DIM), lambda b, h, i: (b, i, h)),
                  pl.BlockSpec((1, m, HEAD_DIM), lambda b, h, i: (b, 0, h)),
                  pl.BlockSpec((1, m, HEAD_DIM), lambda b, h, i: (b, 0, h))],
        out_specs=pl.BlockSpec((1, tq, HEAD_DIM), lambda b, h, i: (b, i, h)),
        out_shape=jax.ShapeDtypeStruct((B, n, G), F32),
        scratch_shapes=[pltpu.VMEM((2, tq, 1), F32), pltpu.VMEM((2, tq, 1), F32),
                        pltpu.VMEM((2, tq, HEAD_DIM), F32)],
        compiler_params=_params(3),
        name="diff_attention",
    )(lam_row, norm_g.reshape(1, HEAD_DIM).astype(F32), q, k, v)


def _moe_kernel(be_ref, nused_ref, x_ref, w1_ref, w3_ref, w2_ref, wr_ref, y_ref):
    i = pl.program_id(0)

    @pl.when(i < nused_ref[0])
    def _():
        x = x_ref[...]
        h1 = jnp.dot(x, w1_ref[0], preferred_element_type=F32)
        h3 = jnp.dot(x, w3_ref[0], preferred_element_type=F32)
        hid = (h1 * jax.nn.sigmoid(h1) * h3).astype(BF16)
        y_ref[...] = jnp.dot(hid, w2_ref[0], preferred_element_type=F32) * wr_ref[...]

    @pl.when(i >= nused_ref[0])
    def _():
        y_ref[...] = jnp.zeros(y_ref.shape, y_ref.dtype)


def grouped_experts(x_sorted, blk_e, n_used, wr, w1, w3, w2, tm):
    R, D = x_sorted.shape
    FF = w1.shape[-1]
    grid_spec = pltpu.PrefetchScalarGridSpec(
        num_scalar_prefetch=2,
        grid=(R // tm,),
        in_specs=[pl.BlockSpec((tm, D), lambda i, be, nu: (i, 0)),
                  pl.BlockSpec((1, D, FF), lambda i, be, nu: (be[i], 0, 0)),
                  pl.BlockSpec((1, D, FF), lambda i, be, nu: (be[i], 0, 0)),
                  pl.BlockSpec((1, FF, D), lambda i, be, nu: (be[i], 0, 0)),
                  pl.BlockSpec((tm, 1), lambda i, be, nu: (i, 0))],
        out_specs=pl.BlockSpec((tm, D), lambda i, be, nu: (i, 0)),
    )
    return pl.pallas_call(
        _moe_kernel,
        grid_spec=grid_spec,
        out_shape=jax.ShapeDtypeStruct((R, D), F32),
        compiler_params=_params(1),
        name="grouped_experts",
    )(blk_e, n_used, x_sorted, w1, w3, w2, wr)


def _shared_kernel(x_ref, w1_ref, w3_ref, w2_ref, y_ref):
    x = x_ref[...].astype(BF16)
    h1 = jnp.dot(x, w1_ref[...], preferred_element_type=F32)
    h3 = jnp.dot(x, w3_ref[...], preferred_element_type=F32)
    hid = (h1 * jax.nn.sigmoid(h1) * h3).astype(BF16)
    y_ref[...] = jnp.dot(hid, w2_ref[...], preferred_element_type=F32)


def shared_expert(tok, w1, w3, w2):
    T, D = tok.shape
    FF = w1.shape[-1]
    tm = _pick_tile(T, (512, 256, 128))
    return pl.pallas_call(
        _shared_kernel,
        grid=(T // tm,),
        in_specs=[pl.BlockSpec((tm, D), lambda i: (i, 0)),
                  pl.BlockSpec((D, FF), lambda i: (0, 0)),
                  pl.BlockSpec((D, FF), lambda i: (0, 0)),
                  pl.BlockSpec((FF, D), lambda i: (0, 0))],
        out_specs=pl.BlockSpec((tm, D), lambda i: (i, 0)),
        out_shape=jax.ShapeDtypeStruct((T, D), F32),
        compiler_params=_params(1),
        name="shared_expert",
    )(tok, w1, w3, w2)


def moe_ffn(tok, router_w, router_bias, w1, w3, w2, sw1, sw3, sw2):
    T, D = tok.shape
    tm = 512 if T * TOP_K >= 64 * 512 else 128
    scores = jax.nn.sigmoid(jnp.matmul(tok, router_w, precision=lax.Precision.HIGHEST))
    _, idx = lax.top_k(scores + router_bias.astype(F32), TOP_K)
    wts = jnp.take_along_axis(scores, idx, -1)
    wts = wts / jnp.sum(wts, -1, keepdims=True) * ROUTED_SCALE
    flat_e = idx.reshape(-1)
    order = jnp.argsort(flat_e)
    e_sorted = flat_e[order]
    counts = jnp.bincount(flat_e, length=N_EXPERTS)
    starts = jnp.cumsum(counts) - counts
    padded = (counts + tm - 1) // tm * tm
    pad_ends = jnp.cumsum(padded)
    dest_sorted = pad_ends[e_sorted] - padded[e_sorted] + jnp.arange(T * TOP_K) - starts[e_sorted]
    n_blocks = -(-(T * TOP_K) // tm) + N_EXPERTS
    R = n_blocks * tm
    dest = jnp.zeros((T * TOP_K,), jnp.int32).at[order].set(dest_sorted.astype(jnp.int32))
    buf_tok = jnp.zeros((R,), jnp.int32).at[dest].set(jnp.repeat(jnp.arange(T, dtype=jnp.int32), TOP_K))
    buf_w = jnp.zeros((R,), F32).at[dest].set(wts.reshape(-1))
    blk_e = jnp.minimum(jnp.searchsorted(pad_ends, jnp.arange(n_blocks) * tm, side='right'),
                        N_EXPERTS - 1).astype(jnp.int32)
    n_used = (pad_ends[-1] // tm).astype(jnp.int32).reshape(1)
    x_sorted = tok.astype(BF16)[buf_tok]
    y_sorted = grouped_experts(x_sorted, blk_e, n_used, buf_w.reshape(R, 1), w1, w3, w2, tm)
    routed = jnp.sum(y_sorted[dest.reshape(T, TOP_K)], axis=1)
    return routed + shared_expert(tok, sw1, sw3, sw2)


def layer_norm(t, g, b, eps=1e-5):
    mu = jnp.mean(t, -1, keepdims=True)
    var = jnp.mean(jnp.square(t - mu), -1, keepdims=True)
    return (t - mu) * lax.rsqrt(var + eps) * g + b


def rms_norm(t, g, eps=1e-6):
    return t * lax.rsqrt(jnp.mean(jnp.square(t), -1, keepdims=True) + eps) * g


def l2norm(t, eps=1e-6):
    return t * lax.rsqrt(jnp.sum(jnp.square(t), -1, keepdims=True) + eps)


def depthwise_conv(t, w, pad_left, pad_right):
    return lax.conv_general_dilated(t, w.T[:, None, :].astype(t.dtype), window_strides=(1,),
                                    padding=[(pad_left, pad_right)],
                                    dimension_numbers=('NWC', 'WIO', 'NWC'),
                                    feature_group_count=t.shape[-1])


def axial_rope(n_lat, dim):
    rows = n_lat // GRID_W
    n_freq = dim // 4
    inv = ROPE_BASE ** (-jnp.arange(n_freq, dtype=F32) / n_freq)
    row_pos = jnp.repeat(jnp.arange(rows, dtype=F32), GRID_W)
    col_pos = (jnp.arange(n_lat) % GRID_W).astype(F32)
    ang = jnp.concatenate([row_pos[:, None] * inv, col_pos[:, None] * inv], -1)
    return jnp.cos(ang), jnp.sin(ang)


def apply_rope(t, cos, sin):
    tf = t.reshape(*t.shape[:-1], -1, 2)
    t1, t2 = tf[..., 0], tf[..., 1]
    out = jnp.stack([t1 * cos - t2 * sin, t1 * sin + t2 * cos], -1)
    return out.reshape(t.shape)


def gdn_chunked(q, k, v, g, beta, s0, with_out):
    B, H, n, _ = q.shape
    C = DN_CHUNK
    nc = n // C
    blk = lambda t: t.reshape(B, H, nc, C, *t.shape[3:])
    q, k, v, g, beta = blk(q), blk(k), blk(v), blk(g), blk(beta)
    g_cum = jnp.cumsum(g, -1)
    incl = jnp.tril(jnp.ones((C, C), dtype=bool))
    decay = jnp.exp(jnp.where(incl, g_cum[..., :, None] - g_cum[..., None, :], -jnp.inf))
    k_beta = k * beta[..., None]
    m = jnp.tril(jnp.einsum('bhcid,bhcjd->bhcij', k_beta, k) * decay, -1)
    eye = jnp.eye(C, dtype=m.dtype)
    t_inv = lax.linalg.triangular_solve(eye + m, jnp.broadcast_to(eye, m.shape), left_side=True, lower=True)
    u = t_inv @ (v * beta[..., None])
    w = t_inv @ (k_beta * jnp.exp(g_cum)[..., None])
    g_last = g_cum[..., -1]
    k_dec = k * jnp.exp(g_last[..., None] - g_cum)[..., None]
    front = lambda t: jnp.moveaxis(t, 2, 0)

    def advance(s, w_c, u_c, k_c, gl_c):
        v_new = u_c - w_c @ s
        s_next = s * jnp.exp(gl_c)[..., None, None] + jnp.swapaxes(k_c, -1, -2) @ v_new
        return s_next, v_new

    if not with_out:
        def step_state(s, inp):
            s_next, _ = advance(s, *inp)
            return s_next, None
        s_fin, _ = lax.scan(step_state, s0, tuple(map(front, (w, u, k_dec, g_last))))
        return s_fin, None

    attn = jnp.einsum('bhcid,bhcjd->bhcij', q, k) * decay
    q_dec = q * jnp.exp(g_cum)[..., None]

    def step(s, inp):
        w_c, u_c, k_c, gl_c, q_c, a_c = inp
        s_next, v_new = advance(s, w_c, u_c, k_c, gl_c)
        return s_next, q_c @ s + a_c @ v_new

    s_fin, o = lax.scan(step, s0, tuple(map(front, (w, u, k_dec, g_last, q_dec, attn))))
    return s_fin, jnp.moveaxis(o, 0, 2).reshape(B, H, n, -1)


def gdn_mixer(p_ctx, p_lat, conv_w, a_log, dt_bias, norm_g, with_ctx):
    G = GROUP_WIDTH

    def prep(p):
        B, n, _ = p.shape
        qkv = jax.nn.silu(depthwise_conv(p[..., :3 * G], conv_w, 2, 1))
        heads = lambda t: t.reshape(B, n, N_HEADS, HEAD_DIM).transpose(0, 2, 1, 3)
        q = l2norm(heads(qkv[..., :G])) * HEAD_DIM ** -0.5
        k = l2norm(heads(qkv[..., G:2 * G]))
        v = heads(qkv[..., 2 * G:])
        gates = p[..., 4 * G:].reshape(B, n, 2, 2, N_HEADS)
        g = -jnp.exp(a_log) * jax.nn.softplus(gates[..., 0, :] + dt_bias)
        beta = jax.nn.sigmoid(gates[..., 1, :])
        return q, k, v, g.transpose(2, 0, 3, 1), beta.transpose(2, 0, 3, 1)

    qc, kc, vc, gc, bc = prep(p_ctx)
    ql, kl, vl, gl, bl = prep(p_lat)
    flip = lambda t: jnp.flip(t, 2)
    s0 = jnp.zeros(qc.shape[:2] + (HEAD_DIM, HEAD_DIM), F32)
    s_fwd, o_c_f = gdn_chunked(qc, kc, vc, gc[0], bc[0], s0, with_ctx)
    s_bwd, o_c_b = gdn_chunked(flip(qc), flip(kc), flip(vc), flip(gc[1]), flip(bc[1]), s0, with_ctx)
    _, o_l_f = gdn_chunked(ql, kl, vl, gl[0], bl[0], s_fwd, True)
    _, o_l_b = gdn_chunked(flip(ql), flip(kl), flip(vl), flip(gl[1]), flip(bl[1]), s_bwd, True)

    def finish(o, p):
        B, _, n, _ = o.shape
        o = rms_norm(o, norm_g).transpose(0, 2, 1, 3).reshape(B, n, G)
        return o * jax.nn.silu(p[..., 3 * G:4 * G])

    out_l = finish(o_l_f + flip(o_l_b), p_lat)
    out_c = finish(o_c_f + flip(o_c_b), p_ctx) if with_ctx else None
    return out_c, out_l


def linear_scan(log_a, b, h0):
    def combine(e1, e2):
        return e1[0] + e2[0], e1[1] * jnp.exp(e2[0]) + e2[1]
    cum_log_a, h = lax.associative_scan(combine, (log_a, b), axis=1)
    return h + jnp.exp(cum_log_a) * h0[:, None, :]


def rglru_mixer(p_ctx, p_lat, conv_w, conv_b, wa, ba, wx, bx, lam, with_ctx):
    W = GROUP_WIDTH
    xc = depthwise_conv(p_ctx[..., :W], conv_w, 2, 1) + conv_b
    xl = depthwise_conv(p_lat[..., :W], conv_w, 2, 1) + conv_b

    def gates(xb, d):
        B, n, _ = xb.shape
        xh = xb.reshape(B, n, N_HEADS, W // N_HEADS)
        r = jax.nn.sigmoid(jnp.einsum('bnhi,hij->bnhj', xh, wa[d]).reshape(B, n, W) + ba[d])
        i = jax.nn.sigmoid(jnp.einsum('bnhi,hij->bnhj', xh, wx[d]).reshape(B, n, W) + bx[d])
        log_a = -LRU_C * r * jax.nn.softplus(-lam[d])
        return log_a, jnp.sqrt(-jnp.expm1(2.0 * log_a)) * (i * xb)

    flip = lambda t: jnp.flip(t, 1)
    h0 = jnp.zeros((xc.shape[0], W), F32)
    la, bb = gates(xc, 0)
    hc_f = linear_scan(la, bb, h0)
    la, bb = gates(xc, 1)
    hc_b = flip(linear_scan(flip(la), flip(bb), h0))
    la, bb = gates(xl, 0)
    hl_f = linear_scan(la, bb, hc_f[:, -1])
    la, bb = gates(xl, 1)
    hl_b = flip(linear_scan(flip(la), flip(bb), hc_b[:, 0]))
    out_l = (hl_f + hl_b) * jax.nn.gelu(p_lat[..., W:])
    out_c = (hc_f + hc_b) * jax.nn.gelu(p_ctx[..., W:]) if with_ctx else None
    return out_c, out_l


def diff_attn_mixer(p_ctx, p_lat, rope_cos, rope_sin, lam_vecs, norm_g, layer_idx, with_ctx):
    G = GROUP_WIDTH

    def split_heads(p):
        B, n, _ = p.shape
        q = p[..., :G].reshape(B, n, N_HEADS, 2, DA_QK)
        k = p[..., G:2 * G].reshape(B, n, N_HEADS, 2, DA_QK)
        return q, k, p[..., 2 * G:]

    qc, kc, vc = split_heads(p_ctx)
    ql, kl, vl = split_heads(p_lat)
    B, n = ql.shape[:2]
    L = qc.shape[1]
    cos = rope_cos[None, :, None, None, :]
    sin = rope_sin[None, :, None, None, :]
    scale = DA_QK ** -0.5
    ql = (apply_rope(ql, cos, sin) * scale).reshape(B, n, G).astype(BF16)
    kl = apply_rope(kl, cos, sin).reshape(B, n, G)
    kc = kc.reshape(B, L, G)
    lam_init = 0.8 - 0.6 * math.exp(-0.3 * layer_idx)
    lv = lam_vecs.astype(F32)
    lam = jnp.exp(jnp.sum(lv[0] * lv[1])) - jnp.exp(jnp.sum(lv[2] * lv[3])) + lam_init
    k_all = jnp.concatenate([kc, kl], 1).astype(BF16)
    v_all = jnp.concatenate([vc, vl], 1).astype(BF16)
    out_l = diff_attention(ql, k_all, v_all, lam, norm_g, 1.0 - lam_init)
    out_c = None
    if with_ctx:
        out_c = diff_attention((qc * scale).reshape(B, L, G).astype(BF16), kc.astype(BF16), vc.astype(BF16),
                               lam, norm_g, 1.0 - lam_init)
    return out_c, out_l


def short_conv_mixer(p, conv_w):
    b_g, c_g, xs = jnp.split(p, 3, -1)
    return b_g * depthwise_conv(c_g * xs, conv_w, 1, 1)


def kernel(x, c, ctx, c_ctx, w_ada, b_ada, w_in, dn_conv, dn_a_log, dn_dt_bias, dn_norm, lru_conv, lru_conv_b, lru_wa, lru_ba, lru_wx, lru_bx, lru_lambda, da_lambda, da_norm, sc_conv, w_out, ln1_g, ln1_b, router_w, router_bias, exp_w1, exp_w3, exp_w2, sh_w1, sh_w3, sh_w2, ln2_g, ln2_b):
    depth = w_in.shape[0]
    alpha = (2 * depth) ** 0.25
    B, N, D = x.shape
    L = ctx.shape[1]
    G = GROUP_WIDTH
    dn_in = 4 * G + 4 * N_HEADS
    col_splits = [dn_in, dn_in + 2 * G, dn_in + 5 * G]
    silu_c = jax.nn.silu(c)[:, None, :]
    silu_cc = jax.nn.silu(c_ctx)[None, None, :]
    rope_cos, rope_sin = axial_rope(N, DA_QK)
    hi = lax.Precision.HIGHEST
    xc = ctx
    for l in range(depth):
        with_ctx = l < depth - 1
        sh1, s1, g1, sh2, s2, g2 = jnp.split(jnp.matmul(silu_c, w_ada[l], precision=hi) + b_ada[l], 6, -1)
        sh1c, s1c, g1c, sh2c, s2c, g2c = jnp.split(jnp.matmul(silu_cc, w_ada[l], precision=hi) + b_ada[l], 6, -1)
        n_in = w_in.shape[-1]
        n_pad = -n_in % 128
        w_in_l = jnp.pad(w_in[l], ((0, 0), (0, n_pad))).astype(BF16)
        a_lat = (x * (1 + s1) + sh1).reshape(B * N, D)
        a_ctx = jnp.broadcast_to(xc * (1 + s1c) + sh1c, (B, L, D)).reshape(B * L, D)
        p_lat = jnp.split(dense_matmul(a_lat, w_in_l)[:, :n_in].reshape(B, N, n_in), col_splits, -1)
        p_ctx = jnp.split(dense_matmul(a_ctx, w_in_l)[:, :n_in].reshape(B, L, n_in), col_splits, -1)
        a_c, a_l = gdn_mixer(p_ctx[0], p_lat[0], dn_conv[l], dn_a_log[l], dn_dt_bias[l], dn_norm[l], with_ctx)
        b_c, b_l = rglru_mixer(p_ctx[1], p_lat[1], lru_conv[l], lru_conv_b[l], lru_wa[l], lru_ba[l],
                               lru_wx[l], lru_bx[l], lru_lambda[l], with_ctx)
        c_c, c_l = diff_attn_mixer(p_ctx[2], p_lat[2], rope_cos, rope_sin, da_lambda[l], da_norm[l], l, with_ctx)
        d_l = short_conv_mixer(p_lat[3], sc_conv[l])
        w_out_l = w_out[l].astype(BF16)
        mix_l = dense_matmul(jnp.concatenate([a_l, b_l, c_l, d_l], -1).reshape(B * N, D), w_out_l).reshape(B, N, D)
        x_mid = layer_norm(alpha * x + g1 * mix_l, ln1_g[l], ln1_b[l])
        moe_args = (router_w[l], router_bias[l], exp_w1[l].astype(BF16), exp_w3[l].astype(BF16),
                    exp_w2[l].astype(BF16), sh_w1[l].astype(BF16), sh_w3[l].astype(BF16), sh_w2[l].astype(BF16))
        if with_ctx:
            d_c = short_conv_mixer(p_ctx[3], sc_conv[l])
            mix_c = dense_matmul(jnp.concatenate([a_c, b_c, c_c, d_c], -1).reshape(B * L, D), w_out_l).reshape(B, L, D)
            xc_mid = layer_norm(alpha * xc + g1c * mix_c, ln1_g[l], ln1_b[l])
            tok = jnp.concatenate([(xc_mid * (1 + s2c) + sh2c).reshape(-1, D),
                                   (x_mid * (1 + s2) + sh2).reshape(-1, D)], 0)
            y = moe_ffn(tok, *moe_args)
            xc = layer_norm(alpha * xc_mid + g2c * y[:B * L].reshape(B, L, D), ln2_g[l], ln2_b[l])
            y_l = y[B * L:].reshape(B, N, D)
        else:
            y_l = moe_ffn((x_mid * (1 + s2) + sh2).reshape(-1, D), *moe_args).reshape(B, N, D)
        x = layer_norm(alpha * x_mid + g2 * y_l, ln2_g[l], ln2_b[l])
    return x
```

```python
import functools
import math

import jax
import jax.numpy as jnp
from jax import lax
from jax.experimental import pallas as pl
from jax.experimental.pallas import tpu as pltpu

F32 = jnp.float32
BF16 = jnp.bfloat16

GRID_W = 64
GROUP_WIDTH = 512
HEAD_DIM = 128
N_HEADS = GROUP_WIDTH // HEAD_DIM
DN_CHUNK = 64
LRU_C = 8.0
DA_QK = HEAD_DIM // 2
ROPE_BASE = 10000.0
N_EXPERTS = 64
TOP_K = 6
ROUTED_SCALE = 2.5

ATTN_TQ = (1024, 512, 256, 128)
ATTN_TK = (640, 512, 256, 128)
ATTN_UNROLL = 2

V7X_VMEM_BYTES = 64 * 1024 * 1024
VMEM_LIMIT = V7X_VMEM_BYTES * 3 // 4


def _params(n_axes):
    return pltpu.CompilerParams(dimension_semantics=("arbitrary",) * n_axes, vmem_limit_bytes=VMEM_LIMIT)


def _pick_tile(n, candidates):
    for c in candidates:
        if n % c == 0:
            return c
    return n


def _mm_kernel(a_ref, w_ref, o_ref):
    o_ref[...] = jnp.dot(a_ref[...].astype(BF16), w_ref[...], preferred_element_type=F32).astype(o_ref.dtype)


def dense_matmul(a, w, out_dtype=F32):
    M, K = a.shape
    N = w.shape[1]
    tm = _pick_tile(M, (1024, 512, 256, 128))
    tn = _pick_tile(N, (512, 256, 128))
    return pl.pallas_call(
        _mm_kernel,
        grid=(M // tm, N // tn),
        in_specs=[pl.BlockSpec((tm, K), lambda i, j: (i, 0)),
                  pl.BlockSpec((K, tn), lambda i, j: (0, j))],
        out_specs=pl.BlockSpec((tm, tn), lambda i, j: (i, j)),
        out_shape=jax.ShapeDtypeStruct((M, N), out_dtype),
        compiler_params=_params(2),
        name="dense_matmul",
    )(a, w)


def _attn_kernel(lam_ref, g_ref, q_ref, kt_ref, v_ref, o_ref, m_scr, acc_scr, *, tk, n_chunks, out_scale):
    q = q_ref[0]
    lane = lax.broadcasted_iota(jnp.int32, q.shape, 1)
    zero = jnp.zeros_like(q)
    q_maps = (jnp.where(lane < DA_QK, q, zero), jnp.where(lane >= DA_QK, q, zero))
    m_scr[...] = jnp.full(m_scr.shape, -jnp.inf, F32)
    acc_scr[...] = jnp.zeros(acc_scr.shape, F32)
    ones = jnp.ones((tk, HEAD_DIM), BF16)

    def chunk(c, carry):
        off = pl.multiple_of(c * tk, tk)
        ktc = kt_ref[0, :, pl.ds(off, tk)]
        v_ext = jnp.concatenate([v_ref[0, pl.ds(off, tk), :], ones], axis=1)
        for mp in range(2):
            s = jnp.dot(q_maps[mp], ktc, preferred_element_type=F32)
            m_prev = m_scr[mp]
            m_new = jnp.maximum(m_prev, jnp.max(s, axis=-1, keepdims=True))
            p = jnp.exp(s - pltpu.repeat(m_new, tk // HEAD_DIM, axis=1))
            alpha = jnp.exp(m_prev - m_new)
            acc_scr[mp] = (pltpu.repeat(alpha, 2, axis=1) * acc_scr[mp]
                           + jnp.dot(p.astype(BF16), v_ext, preferred_element_type=F32))
            m_scr[mp] = m_new
        return carry

    lax.fori_loop(0, n_chunks, chunk, 0, unroll=ATTN_UNROLL)
    a0, a1 = acc_scr[0], acc_scr[1]
    o = a0[:, :HEAD_DIM] / a0[:, HEAD_DIM:] - lam_ref[...] * (a1[:, :HEAD_DIM] / a1[:, HEAD_DIM:])
    o = o * lax.rsqrt(jnp.mean(o * o, axis=-1, keepdims=True) + 1e-6)
    o_ref[0] = o * g_ref[...] * out_scale


def diff_attention(q, kt, v, lam, norm_g, out_scale):
    B, n, G = q.shape
    m = v.shape[1]
    tq = _pick_tile(n, ATTN_TQ)
    tk = _pick_tile(m, ATTN_TK)
    kern = functools.partial(_attn_kernel, tk=tk, n_chunks=m // tk, out_scale=out_scale)
    lam_row = jnp.full((1, HEAD_DIM), lam, F32)
    return pl.pallas_call(
        kern,
        grid=(B, N_HEADS, n // tq),
        in_specs=[pl.BlockSpec((1, HEAD_DIM), lambda b, h, i: (0, 0)),
                  pl.BlockSpec((1, HEAD_DIM), lambda b, h, i: (0, 0)),
                  pl.BlockSpec((1, tq, HEAD_DIM), lambda b, h, i: (b, i, h)),
                  pl.BlockSpec((1, HEAD_DIM, m), lambda b, h, i: (b, h, 0)),
                  pl.BlockSpec((1, m, HEAD_DIM), lambda b, h, i: (b, 0, h))],
        out_specs=pl.BlockSpec((1, tq, HEAD_DIM), lambda b, h, i: (b, i, h)),
        out_shape=jax.ShapeDtypeStruct((B, n, G), F32),
        scratch_shapes=[pltpu.VMEM((2, tq, HEAD_DIM), F32), pltpu.VMEM((2, tq, 2 * HEAD_DIM), F32)],
        compiler_params=_params(3),
        name="diff_attention",
    )(lam_row, norm_g.reshape(1, HEAD_DIM).astype(F32), q, kt, v)


def _lru_kernel(h0_ref, xprev_ref, x_ref, xnext_ref, cw_ref, cb_ref, wg_ref, bg_ref, sp_ref, *rest,
                tb, n_blocks, reverse, finish):
    if finish:
        hother_ref, gate_ref, out_ref, state_ref, a_scr, b_scr, h_scr = rest
    else:
        out_ref, state_ref, a_scr, b_scr, h_scr = rest
    j = pl.program_id(1)
    t_blk = (n_blocks - 1 - j) if reverse else j
    W = x_ref.shape[-1]

    @pl.when(j == 0)
    def _():
        h_scr[...] = jnp.broadcast_to(h0_ref[0], h_scr.shape)

    prev = jnp.where(t_blk > 0, xprev_ref[0], 0.0)
    nxt = jnp.where(t_blk < n_blocks - 1, xnext_ref[0], 0.0)
    x_ext = jnp.concatenate([prev, x_ref[0], nxt], axis=0)
    cw = cw_ref[...]
    xb = cb_ref[...] + sum(cw[k:k + 1] * x_ext[6 + k:6 + k + tb] for k in range(4))
    xb16 = xb.astype(BF16)
    hw = W // N_HEADS
    pre = [jnp.dot(xb16[:, h * hw:(h + 1) * hw], wg_ref[h], preferred_element_type=F32) for h in range(N_HEADS)]
    pre_r = jnp.concatenate([p[:, :hw] for p in pre], axis=1) + bg_ref[0:1]
    pre_i = jnp.concatenate([p[:, hw:] for p in pre], axis=1) + bg_ref[1:2]
    log_a = -LRU_C * jax.nn.sigmoid(pre_r) * sp_ref[...]
    a = jnp.exp(log_a)
    a_scr[...] = a
    b_scr[...] = jnp.sqrt(-jnp.tanh(log_a) * (a * a + 1.0)) * (jax.nn.sigmoid(pre_i) * xb)

    row = lax.broadcasted_iota(jnp.int32, (8, W), 0)
    n_rows = tb // 8

    def block(i, h):
        blk = (n_rows - 1 - i) if reverse else i
        r0 = pl.multiple_of(blk * 8, 8)
        a = a_scr[pl.ds(r0, 8), :]
        b = b_scr[pl.ds(r0, 8), :]
        for d in (1, 2, 4):
            if reverse:
                keep = row < 8 - d
                a_sh, b_sh = pltpu.roll(a, 8 - d, 0), pltpu.roll(b, 8 - d, 0)
            else:
                keep = row >= d
                a_sh, b_sh = pltpu.roll(a, d, 0), pltpu.roll(b, d, 0)
            b = jnp.where(keep, b + a * b_sh, b)
            a = jnp.where(keep, a * a_sh, a)
        hb = b + a * h
        b_scr[pl.ds(r0, 8), :] = hb
        last = hb[0:1] if reverse else hb[7:8]
        return jnp.broadcast_to(last, (8, W))

    h_fin = lax.fori_loop(0, n_rows, block, h_scr[...], unroll=4)
    h_scr[...] = h_fin
    state_ref[0] = h_fin
    if finish:
        out_ref[0] = (hother_ref[0] + b_scr[...]) * jax.nn.gelu(gate_ref[0])
    else:
        out_ref[0] = b_scr[...]


def lru_scan(p, x_col, gate_col, h0, cw, cb, wg, bg, sp, reverse, h_other=None):
    B, n, _ = p.shape
    W = GROUP_WIDTH
    tb = _pick_tile(n, (512, 256, 128))
    nb = n // tb
    finish = h_other is not None
    tix = (lambda j: nb - 1 - j) if reverse else (lambda j: j)
    r8 = tb // 8
    in_specs = [pl.BlockSpec((1, 8, W), lambda b, j: (b, 0, 0)),
                pl.BlockSpec((1, 8, W), lambda b, j: (b, jnp.maximum(tix(j) * r8 - 1, 0), x_col)),
                pl.BlockSpec((1, tb, W), lambda b, j: (b, tix(j), x_col)),
                pl.BlockSpec((1, 8, W), lambda b, j: (b, jnp.minimum((tix(j) + 1) * r8, n // 8 - 1), x_col)),
                pl.BlockSpec((4, W), lambda b, j: (0, 0)),
                pl.BlockSpec((1, W), lambda b, j: (0, 0)),
                pl.BlockSpec((N_HEADS, W // N_HEADS, 2 * W // N_HEADS), lambda b, j: (0, 0, 0)),
                pl.BlockSpec((2, W), lambda b, j: (0, 0)),
                pl.BlockSpec((1, W), lambda b, j: (0, 0))]
    args = [h0, p, p, p, cw, cb, wg, bg, sp]
    if finish:
        in_specs += [pl.BlockSpec((1, tb, W), lambda b, j: (b, tix(j), 0)),
                     pl.BlockSpec((1, tb, W), lambda b, j: (b, tix(j), gate_col))]
        args += [h_other, p]
    kern = functools.partial(_lru_kernel, tb=tb, n_blocks=nb, reverse=reverse, finish=finish)
    return pl.pallas_call(
        kern,
        grid=(B, nb),
        in_specs=in_specs,
        out_specs=[pl.BlockSpec((1, tb, W), lambda b, j: (b, tix(j), 0)),
                   pl.BlockSpec((1, 8, W), lambda b, j: (b, 0, 0))],
        out_shape=[jax.ShapeDtypeStruct((B, n, W), F32), jax.ShapeDtypeStruct((B, 8, W), F32)],
        scratch_shapes=[pltpu.VMEM((tb, W), F32), pltpu.VMEM((tb, W), F32), pltpu.VMEM((8, W), F32)],
        compiler_params=_params(2),
        name="lru_scan",
    )(*args)


def rglru_mixer(p_ctx, p_lat, x_col, gate_col, conv_w, conv_b, wa, ba, wx, bx, lam):
    B = p_lat.shape[0]
    W = GROUP_WIDTH
    cw = conv_w.T.astype(F32)
    cb = conv_b.reshape(1, W).astype(F32)
    h0 = jnp.zeros((B, 8, W), F32)
    outs = []
    states = {}
    for d, reverse in ((0, False), (1, True)):
        wg = jnp.concatenate([wa[d], wx[d]], axis=-1).astype(BF16)
        bg = jnp.stack([ba[d], bx[d]]).astype(F32)
        sp = jax.nn.softplus(-lam[d]).reshape(1, W).astype(F32)
        states[d] = (wg, bg, sp)
    wg, bg, sp = states[0]
    hc_f, sc_f = lru_scan(p_ctx, x_col, gate_col, h0, cw, cb, wg, bg, sp, False)
    hl_f, _ = lru_scan(p_lat, x_col, gate_col, sc_f, cw, cb, wg, bg, sp, False)
    wg, bg, sp = states[1]
    out_c, sc_b = lru_scan(p_ctx, x_col, gate_col, h0, cw, cb, wg, bg, sp, True, h_other=hc_f)
    out_l, _ = lru_scan(p_lat, x_col, gate_col, sc_b, cw, cb, wg, bg, sp, True, h_other=hl_f)
    return out_c, out_l


def _moe_kernel(be_ref, nused_ref, x_ref, w1_ref, w3_ref, w2_ref, wr_ref, y_ref):
    i = pl.program_id(0)

    @pl.when(i < nused_ref[0])
    def _():
        x = x_ref[...]
        h1 = jnp.dot(x, w1_ref[0], preferred_element_type=F32)
        h3 = jnp.dot(x, w3_ref[0], preferred_element_type=F32)
        hid = (h1 * jax.nn.sigmoid(h1) * h3).astype(BF16)
        y_ref[...] = jnp.dot(hid, w2_ref[0], preferred_element_type=F32) * wr_ref[...]

    @pl.when(i >= nused_ref[0])
    def _():
        y_ref[...] = jnp.zeros(y_ref.shape, y_ref.dtype)


def grouped_experts(x_sorted, blk_e, n_used, wr, w1, w3, w2, tm):
    R, D = x_sorted.shape
    FF = w1.shape[-1]
    grid_spec = pltpu.PrefetchScalarGridSpec(
        num_scalar_prefetch=2,
        grid=(R // tm,),
        in_specs=[pl.BlockSpec((tm, D), lambda i, be, nu: (i, 0)),
                  pl.BlockSpec((1, D, FF), lambda i, be, nu: (be[i], 0, 0)),
                  pl.BlockSpec((1, D, FF), lambda i, be, nu: (be[i], 0, 0)),
                  pl.BlockSpec((1, FF, D), lambda i, be, nu: (be[i], 0, 0)),
                  pl.BlockSpec((tm, 1), lambda i, be, nu: (i, 0))],
        out_specs=pl.BlockSpec((tm, D), lambda i, be, nu: (i, 0)),
    )
    return pl.pallas_call(
        _moe_kernel,
        grid_spec=grid_spec,
        out_shape=jax.ShapeDtypeStruct((R, D), F32),
        compiler_params=_params(1),
        name="grouped_experts",
    )(blk_e, n_used, x_sorted, w1, w3, w2, wr)


def _shared_kernel(x_ref, w1_ref, w3_ref, w2_ref, y_ref):
    x = x_ref[...].astype(BF16)
    h1 = jnp.dot(x, w1_ref[...], preferred_element_type=F32)
    h3 = jnp.dot(x, w3_ref[...], preferred_element_type=F32)
    hid = (h1 * jax.nn.sigmoid(h1) * h3).astype(BF16)
    y_ref[...] = jnp.dot(hid, w2_ref[...], preferred_element_type=F32)


def shared_expert(tok, w1, w3, w2):
    T, D = tok.shape
    FF = w1.shape[-1]
    tm = _pick_tile(T, (512, 256, 128))
    return pl.pallas_call(
        _shared_kernel,
        grid=(T // tm,),
        in_specs=[pl.BlockSpec((tm, D), lambda i: (i, 0)),
                  pl.BlockSpec((D, FF), lambda i: (0, 0)),
                  pl.BlockSpec((D, FF), lambda i: (0, 0)),
                  pl.BlockSpec((FF, D), lambda i: (0, 0))],
        out_specs=pl.BlockSpec((tm, D), lambda i: (i, 0)),
        out_shape=jax.ShapeDtypeStruct((T, D), F32),
        compiler_params=_params(1),
        name="shared_expert",
    )(tok, w1, w3, w2)


def moe_ffn(tok, router_w, router_bias, w1, w3, w2, sw1, sw3, sw2):
    T, D = tok.shape
    tm = 512 if T * TOP_K >= 64 * 512 else 128
    scores = jax.nn.sigmoid(jnp.matmul(tok, router_w, preferred_element_type=F32))
    _, idx = lax.top_k(scores + router_bias.astype(F32), TOP_K)
    wts = jnp.take_along_axis(scores, idx, -1)
    wts = wts / jnp.sum(wts, -1, keepdims=True) * ROUTED_SCALE
    flat_e = idx.reshape(-1)
    order = jnp.argsort(flat_e)
    e_sorted = flat_e[order]
    counts = jnp.bincount(flat_e, length=N_EXPERTS)
    starts = jnp.cumsum(counts) - counts
    padded = (counts + tm - 1) // tm * tm
    pad_ends = jnp.cumsum(padded)
    dest_sorted = pad_ends[e_sorted] - padded[e_sorted] + jnp.arange(T * TOP_K) - starts[e_sorted]
    n_blocks = -(-(T * TOP_K) // tm) + N_EXPERTS
    R = n_blocks * tm
    dest = jnp.zeros((T * TOP_K,), jnp.int32).at[order].set(dest_sorted.astype(jnp.int32))
    buf_tok = jnp.zeros((R,), jnp.int32).at[dest].set(jnp.repeat(jnp.arange(T, dtype=jnp.int32), TOP_K))
    buf_w = jnp.zeros((R,), F32).at[dest].set(wts.reshape(-1))
    blk_e = jnp.minimum(jnp.searchsorted(pad_ends, jnp.arange(n_blocks) * tm, side='right'),
                        N_EXPERTS - 1).astype(jnp.int32)
    n_used = (pad_ends[-1] // tm).astype(jnp.int32).reshape(1)
    x_sorted = tok.astype(BF16)[buf_tok]
    y_sorted = grouped_experts(x_sorted, blk_e, n_used, buf_w.reshape(R, 1), w1, w3, w2, tm)
    routed = jnp.sum(y_sorted[dest.reshape(T, TOP_K)], axis=1)
    return routed + shared_expert(tok, sw1, sw3, sw2)


def layer_norm(t, g, b, eps=1e-5):
    mu = jnp.mean(t, -1, keepdims=True)
    var = jnp.mean(jnp.square(t - mu), -1, keepdims=True)
    return (t - mu) * lax.rsqrt(var + eps) * g + b


def rms_norm(t, g, eps=1e-6):
    return t * lax.rsqrt(jnp.mean(jnp.square(t), -1, keepdims=True) + eps) * g


def l2norm(t, eps=1e-6):
    return t * lax.rsqrt(jnp.sum(jnp.square(t), -1, keepdims=True) + eps)


def depthwise_conv(t, w, pad_left, pad_right):
    return lax.conv_general_dilated(t, w.T[:, None, :].astype(t.dtype), window_strides=(1,),
                                    padding=[(pad_left, pad_right)],
                                    dimension_numbers=('NWC', 'WIO', 'NWC'),
                                    feature_group_count=t.shape[-1])


def axial_rope(n_lat, dim):
    rows = n_lat // GRID_W
    n_freq = dim // 4
    inv = ROPE_BASE ** (-jnp.arange(n_freq, dtype=F32) / n_freq)
    row_pos = jnp.repeat(jnp.arange(rows, dtype=F32), GRID_W)
    col_pos = (jnp.arange(n_lat) % GRID_W).astype(F32)
    ang = jnp.concatenate([row_pos[:, None] * inv, col_pos[:, None] * inv], -1)
    return jnp.cos(ang), jnp.sin(ang)


def apply_rope(t, cos, sin):
    tf = t.reshape(*t.shape[:-1], -1, 2)
    t1, t2 = tf[..., 0], tf[..., 1]
    out = jnp.stack([t1 * cos - t2 * sin, t1 * sin + t2 * cos], -1)
    return out.reshape(t.shape)


def gdn_chunked(q, k, v, g, beta, s0, with_out):
    B, H, n, _ = q.shape
    C = DN_CHUNK
    nc = n // C
    blk = lambda t: t.reshape(B, H, nc, C, *t.shape[3:])
    q, k, v, g, beta = blk(q), blk(k), blk(v), blk(g), blk(beta)
    g_cum = jnp.cumsum(g, -1)
    incl = jnp.tril(jnp.ones((C, C), dtype=bool))
    decay = jnp.exp(jnp.where(incl, g_cum[..., :, None] - g_cum[..., None, :], -jnp.inf))
    k_beta = k * beta[..., None]
    m = jnp.tril(jnp.einsum('bhcid,bhcjd->bhcij', k_beta, k) * decay, -1)
    eye = jnp.eye(C, dtype=m.dtype)
    t_inv = lax.linalg.triangular_solve(eye + m, jnp.broadcast_to(eye, m.shape), left_side=True, lower=True)
    u = t_inv @ (v * beta[..., None])
    w = t_inv @ (k_beta * jnp.exp(g_cum)[..., None])
    g_last = g_cum[..., -1]
    k_dec = k * jnp.exp(g_last[..., None] - g_cum)[..., None]
    front = lambda t: jnp.moveaxis(t, 2, 0)

    def advance(s, w_c, u_c, k_c, gl_c):
        v_new = u_c - w_c @ s
        s_next = s * jnp.exp(gl_c)[..., None, None] + jnp.swapaxes(k_c, -1, -2) @ v_new
        return s_next, v_new

    if not with_out:
        def step_state(s, inp):
            s_next, _ = advance(s, *inp)
            return s_next, None
        s_fin, _ = lax.scan(step_state, s0, tuple(map(front, (w, u, k_dec, g_last))))
        return s_fin, None

    attn = jnp.einsum('bhcid,bhcjd->bhcij', q, k) * decay
    q_dec = q * jnp.exp(g_cum)[..., None]

    def step(s, inp):
        w_c, u_c, k_c, gl_c, q_c, a_c = inp
        s_next, v_new = advance(s, w_c, u_c, k_c, gl_c)
        return s_next, q_c @ s + a_c @ v_new

    s_fin, o = lax.scan(step, s0, tuple(map(front, (w, u, k_dec, g_last, q_dec, attn))))
    return s_fin, jnp.moveaxis(o, 0, 2).reshape(B, H, n, -1)


def gdn_mixer(p_ctx, p_lat, conv_w, a_log, dt_bias, norm_g, with_ctx):
    G = GROUP_WIDTH

    def prep(p):
        B, n, _ = p.shape
        qkv = jax.nn.silu(depthwise_conv(p[..., :3 * G], conv_w, 2, 1))
        heads = lambda t: t.reshape(B, n, N_HEADS, HEAD_DIM).transpose(0, 2, 1, 3)
        q = l2norm(heads(qkv[..., :G])) * HEAD_DIM ** -0.5
        k = l2norm(heads(qkv[..., G:2 * G]))
        v = heads(qkv[..., 2 * G:])
        gates = p[..., 4 * G:].reshape(B, n, 2, 2, N_HEADS)
        g = -jnp.exp(a_log) * jax.nn.softplus(gates[..., 0, :] + dt_bias)
        beta = jax.nn.sigmoid(gates[..., 1, :])
        return q, k, v, g.transpose(2, 0, 3, 1), beta.transpose(2, 0, 3, 1)

    qc, kc, vc, gc, bc = prep(p_ctx)
    ql, kl, vl, gl, bl = prep(p_lat)
    flip = lambda t: jnp.flip(t, 2)
    s0 = jnp.zeros(qc.shape[:2] + (HEAD_DIM, HEAD_DIM), F32)
    s_fwd, o_c_f = gdn_chunked(qc, kc, vc, gc[0], bc[0], s0, with_ctx)
    s_bwd, o_c_b = gdn_chunked(flip(qc), flip(kc), flip(vc), flip(gc[1]), flip(bc[1]), s0, with_ctx)
    _, o_l_f = gdn_chunked(ql, kl, vl, gl[0], bl[0], s_fwd, True)
    _, o_l_b = gdn_chunked(flip(ql), flip(kl), flip(vl), flip(gl[1]), flip(bl[1]), s_bwd, True)

    def finish(o, p):
        B, _, n, _ = o.shape
        o = rms_norm(o, norm_g).transpose(0, 2, 1, 3).reshape(B, n, G)
        return o * jax.nn.silu(p[..., 3 * G:4 * G])

    out_l = finish(o_l_f + flip(o_l_b), p_lat)
    out_c = finish(o_c_f + flip(o_c_b), p_ctx) if with_ctx else None
    return out_c, out_l


def diff_attn_mixer(p_ctx, p_lat, rope_cos, rope_sin, lam_vecs, norm_g, layer_idx, with_ctx):
    G = GROUP_WIDTH

    def split_heads(p):
        B, n, _ = p.shape
        q = p[..., :G].reshape(B, n, N_HEADS, 2, DA_QK)
        k = p[..., G:2 * G].reshape(B, n, N_HEADS, 2, DA_QK)
        return q, k, p[..., 2 * G:]

    qc, kc, vc = split_heads(p_ctx)
    ql, kl, vl = split_heads(p_lat)
    B, n = ql.shape[:2]
    L = qc.shape[1]
    cos = rope_cos[None, :, None, None, :]
    sin = rope_sin[None, :, None, None, :]
    scale = DA_QK ** -0.5
    ql = (apply_rope(ql, cos, sin) * scale).reshape(B, n, G).astype(BF16)
    kl = apply_rope(kl, cos, sin).reshape(B, n, G)
    kc = kc.reshape(B, L, G)
    lam_init = 0.8 - 0.6 * math.exp(-0.3 * layer_idx)
    lv = lam_vecs.astype(F32)
    lam = jnp.exp(jnp.sum(lv[0] * lv[1])) - jnp.exp(jnp.sum(lv[2] * lv[3])) + lam_init
    k_all = jnp.concatenate([kc, kl], 1).astype(BF16)
    v_all = jnp.concatenate([vc, vl], 1).astype(BF16)
    out_l = diff_attention(ql, jnp.swapaxes(k_all, 1, 2), v_all, lam, norm_g, 1.0 - lam_init)
    out_c = None
    if with_ctx:
        out_c = diff_attention((qc * scale).reshape(B, L, G).astype(BF16), jnp.swapaxes(kc.astype(BF16), 1, 2),
                               vc.astype(BF16), lam, norm_g, 1.0 - lam_init)
    return out_c, out_l


def short_conv_mixer(p, conv_w):
    b_g, c_g, xs = jnp.split(p, 3, -1)
    return b_g * depthwise_conv(c_g * xs, conv_w, 1, 1)


def kernel(x, c, ctx, c_ctx, w_ada, b_ada, w_in, dn_conv, dn_a_log, dn_dt_bias, dn_norm, lru_conv, lru_conv_b, lru_wa, lru_ba, lru_wx, lru_bx, lru_lambda, da_lambda, da_norm, sc_conv, w_out, ln1_g, ln1_b, router_w, router_bias, exp_w1, exp_w3, exp_w2, sh_w1, sh_w3, sh_w2, ln2_g, ln2_b):
    depth = w_in.shape[0]
    alpha = (2 * depth) ** 0.25
    B, N, D = x.shape
    L = ctx.shape[1]
    G = GROUP_WIDTH
    silu_c = jax.nn.silu(c)[:, None, :]
    silu_cc = jax.nn.silu(c_ctx)[None, None, :]
    rope_cos, rope_sin = axial_rope(N, DA_QK)
    xc = ctx
    n_gate = 4 * N_HEADS
    n_main = w_in.shape[-1] - n_gate
    for l in range(depth):
        with_ctx = l < depth - 1
        sh1, s1, g1, sh2, s2, g2 = jnp.split(silu_c @ w_ada[l] + b_ada[l], 6, -1)
        sh1c, s1c, g1c, sh2c, s2c, g2c = jnp.split(silu_cc @ w_ada[l] + b_ada[l], 6, -1)
        w_l = w_in[l]
        w_in_l = jnp.concatenate([w_l[:, :4 * G], w_l[:, 4 * G + n_gate:], w_l[:, 4 * G:4 * G + n_gate],
                                  jnp.zeros((D, 128 - n_gate), w_l.dtype)], axis=1).astype(BF16)
        a_lat = (x * (1 + s1) + sh1).reshape(B * N, D)
        a_ctx = jnp.broadcast_to(xc * (1 + s1c) + sh1c, (B, L, D)).reshape(B * L, D)
        pl_all = dense_matmul(a_lat, w_in_l).reshape(B, N, -1)
        pc_all = dense_matmul(a_ctx, w_in_l).reshape(B, L, -1)
        dn_cols = lambda p: jnp.concatenate([p[..., :4 * G], p[..., n_main:n_main + n_gate]], -1)
        a_c, a_l = gdn_mixer(dn_cols(pc_all), dn_cols(pl_all), dn_conv[l], dn_a_log[l], dn_dt_bias[l], dn_norm[l],
                             with_ctx)
        b_c, b_l = rglru_mixer(pc_all, pl_all, 4, 5, lru_conv[l], lru_conv_b[l], lru_wa[l], lru_ba[l],
                               lru_wx[l], lru_bx[l], lru_lambda[l])
        c_c, c_l = diff_attn_mixer(pc_all[..., 6 * G:9 * G], pl_all[..., 6 * G:9 * G], rope_cos, rope_sin,
                                   da_lambda[l], da_norm[l], l, with_ctx)
        p_lat = (None, None, None, pl_all[..., 9 * G:12 * G])
        p_ctx = (None, None, None, pc_all[..., 9 * G:12 * G])
        d_l = short_conv_mixer(p_lat[3], sc_conv[l])
        w_out_l = w_out[l].astype(BF16)
        mix_l = dense_matmul(jnp.concatenate([a_l, b_l, c_l, d_l], -1).reshape(B * N, D), w_out_l).reshape(B, N, D)
        x_mid = layer_norm(alpha * x + g1 * mix_l, ln1_g[l], ln1_b[l])
        moe_args = (router_w[l], router_bias[l], exp_w1[l].astype(BF16), exp_w3[l].astype(BF16),
                    exp_w2[l].astype(BF16), sh_w1[l].astype(BF16), sh_w3[l].astype(BF16), sh_w2[l].astype(BF16))
        if with_ctx:
            d_c = short_conv_mixer(p_ctx[3], sc_conv[l])
            mix_c = dense_matmul(jnp.concatenate([a_c, b_c, c_c, d_c], -1).reshape(B * L, D), w_out_l).reshape(B, L, D)
            xc_mid = layer_norm(alpha * xc + g1c * mix_c, ln1_g[l], ln1_b[l])
            tok = jnp.concatenate([(xc_mid * (1 + s2c) + sh2c).reshape(-1, D),
                                   (x_mid * (1 + s2) + sh2).reshape(-1, D)], 0)
            y = moe_ffn(tok, *moe_args)
            xc = layer_norm(alpha * xc_mid + g2c * y[:B * L].reshape(B, L, D), ln2_g[l], ln2_b[l])
            y_l = y[B * L:].reshape(B, N, D)
        else:
            y_l = moe_ffn((x_mid * (1 + s2) + sh2).reshape(-1, D), *moe_args).reshape(B, N, D)
        x = layer_norm(alpha * x_mid + g2 * y_l, ln2_g[l], ln2_b[l])
    return x
```

```python
import functools
import math

import jax
import jax.numpy as jnp
from jax import lax
from jax.experimental import pallas as pl
from jax.experimental.pallas import tpu as pltpu

F32 = jnp.float32
BF16 = jnp.bfloat16

GRID_W = 64
GROUP_WIDTH = 512
HEAD_DIM = 128
N_HEADS = GROUP_WIDTH // HEAD_DIM
DN_CHUNK = 64
LRU_C = 8.0
DA_QK = HEAD_DIM // 2
ROPE_BASE = 10000.0
N_EXPERTS = 64
TOP_K = 6
ROUTED_SCALE = 2.5

ATTN_TQ = (1024, 512, 256, 128)
ATTN_TK = (640, 512, 256, 128)
ATTN_UNROLL = 2

V7X_VMEM_BYTES = 64 * 1024 * 1024
VMEM_LIMIT = V7X_VMEM_BYTES * 3 // 4


def _params(n_axes):
    return pltpu.CompilerParams(dimension_semantics=("arbitrary",) * n_axes, vmem_limit_bytes=VMEM_LIMIT)


def _pick_tile(n, candidates):
    for c in candidates:
        if n % c == 0:
            return c
    return n


def _mm_kernel(a_ref, w_ref, o_ref):
    o_ref[...] = jnp.dot(a_ref[...].astype(BF16), w_ref[...], preferred_element_type=F32).astype(o_ref.dtype)


def dense_matmul(a, w, out_dtype=F32):
    M, K = a.shape
    N = w.shape[1]
    tm = _pick_tile(M, (1024, 512, 256, 128))
    tn = _pick_tile(N, (512, 256, 128))
    return pl.pallas_call(
        _mm_kernel,
        grid=(M // tm, N // tn),
        in_specs=[pl.BlockSpec((tm, K), lambda i, j: (i, 0)),
                  pl.BlockSpec((K, tn), lambda i, j: (0, j))],
        out_specs=pl.BlockSpec((tm, tn), lambda i, j: (i, j)),
        out_shape=jax.ShapeDtypeStruct((M, N), out_dtype),
        compiler_params=_params(2),
        name="dense_matmul",
    )(a, w)


def _attn_kernel(lam_ref, g_ref, q_ref, kt_ref, v_ref, o_ref, m_scr, acc_scr, *, tk, n_chunks, out_scale):
    q = q_ref[0]
    lane = lax.broadcasted_iota(jnp.int32, q.shape, 1)
    zero = jnp.zeros_like(q)
    q_maps = (jnp.where(lane < DA_QK, q, zero), jnp.where(lane >= DA_QK, q, zero))
    m_scr[...] = jnp.full(m_scr.shape, -jnp.inf, F32)
    acc_scr[...] = jnp.zeros(acc_scr.shape, F32)
    ones = jnp.ones((tk, HEAD_DIM), BF16)

    def chunk(c, carry):
        off = pl.multiple_of(c * tk, tk)
        ktc = kt_ref[0, :, pl.ds(off, tk)]
        v_ext = jnp.concatenate([v_ref[0, pl.ds(off, tk), :], ones], axis=1)
        for mp in range(2):
            s = jnp.dot(q_maps[mp], ktc, preferred_element_type=F32)
            m_prev = m_scr[mp]
            m_new = jnp.maximum(m_prev, jnp.max(s, axis=-1, keepdims=True))
            p = jnp.exp(s - pltpu.repeat(m_new, tk // HEAD_DIM, axis=1))
            alpha = jnp.exp(m_prev - m_new)
            acc_scr[mp] = (pltpu.repeat(alpha, 2, axis=1) * acc_scr[mp]
                           + jnp.dot(p.astype(BF16), v_ext, preferred_element_type=F32))
            m_scr[mp] = m_new
        return carry

    lax.fori_loop(0, n_chunks, chunk, 0, unroll=ATTN_UNROLL)
    a0, a1 = acc_scr[0], acc_scr[1]
    o = a0[:, :HEAD_DIM] / a0[:, HEAD_DIM:] - lam_ref[...] * (a1[:, :HEAD_DIM] / a1[:, HEAD_DIM:])
    o = o * lax.rsqrt(jnp.mean(o * o, axis=-1, keepdims=True) + 1e-6)
    o_ref[0] = o * g_ref[...] * out_scale


def diff_attention(q, kt, v, lam, norm_g, out_scale):
    B, n, G = q.shape
    m = v.shape[1]
    tq = _pick_tile(n, ATTN_TQ)
    tk = _pick_tile(m, ATTN_TK)
    kern = functools.partial(_attn_kernel, tk=tk, n_chunks=m // tk, out_scale=out_scale)
    lam_row = jnp.full((1, HEAD_DIM), lam, F32)
    return pl.pallas_call(
        kern,
        grid=(B, N_HEADS, n // tq),
        in_specs=[pl.BlockSpec((1, HEAD_DIM), lambda b, h, i: (0, 0)),
                  pl.BlockSpec((1, HEAD_DIM), lambda b, h, i: (0, 0)),
                  pl.BlockSpec((1, tq, HEAD_DIM), lambda b, h, i: (b, i, h)),
                  pl.BlockSpec((1, HEAD_DIM, m), lambda b, h, i: (b, h, 0)),
                  pl.BlockSpec((1, m, HEAD_DIM), lambda b, h, i: (b, 0, h))],
        out_specs=pl.BlockSpec((1, tq, HEAD_DIM), lambda b, h, i: (b, i, h)),
        out_shape=jax.ShapeDtypeStruct((B, n, G), F32),
        scratch_shapes=[pltpu.VMEM((2, tq, HEAD_DIM), F32), pltpu.VMEM((2, tq, 2 * HEAD_DIM), F32)],
        compiler_params=_params(3),
        name="diff_attention",
    )(lam_row, norm_g.reshape(1, HEAD_DIM).astype(F32), q, kt, v)


def _lru_kernel(h0_ref, xprev_ref, x_ref, xnext_ref, cw_ref, cb_ref, wg_ref, bg_ref, sp_ref, *rest,
                tb, n_blocks, reverse, finish):
    if finish:
        hother_ref, gate_ref, out_ref, state_ref, a_scr, b_scr, h_scr = rest
    else:
        out_ref, state_ref, a_scr, b_scr, h_scr = rest
    j = pl.program_id(1)
    t_blk = (n_blocks - 1 - j) if reverse else j
    W = x_ref.shape[-1]

    @pl.when(j == 0)
    def _():
        h_scr[...] = jnp.broadcast_to(h0_ref[0], h_scr.shape)

    prev = jnp.where(t_blk > 0, xprev_ref[0], 0.0)
    nxt = jnp.where(t_blk < n_blocks - 1, xnext_ref[0], 0.0)
    x_ext = jnp.concatenate([prev, x_ref[0], nxt], axis=0)
    cw = cw_ref[...]
    xb = cb_ref[...] + sum(cw[k:k + 1] * x_ext[6 + k:6 + k + tb] for k in range(4))
    xb16 = xb.astype(BF16)
    hw = W // N_HEADS
    pre = [jnp.dot(xb16[:, h * hw:(h + 1) * hw], wg_ref[h], preferred_element_type=F32) for h in range(N_HEADS)]
    pre_r = jnp.concatenate([p[:, :hw] for p in pre], axis=1) + bg_ref[0:1]
    pre_i = jnp.concatenate([p[:, hw:] for p in pre], axis=1) + bg_ref[1:2]
    log_a = -LRU_C * jax.nn.sigmoid(pre_r) * sp_ref[...]
    a = jnp.exp(log_a)
    a_scr[...] = a
    b_scr[...] = jnp.sqrt(-jnp.tanh(log_a) * (a * a + 1.0)) * (jax.nn.sigmoid(pre_i) * xb)

    row = lax.broadcasted_iota(jnp.int32, (8, W), 0)
    n_rows = tb // 8

    def block(i, h):
        blk = (n_rows - 1 - i) if reverse else i
        r0 = pl.multiple_of(blk * 8, 8)
        a = a_scr[pl.ds(r0, 8), :]
        b = b_scr[pl.ds(r0, 8), :]
        for d in (1, 2, 4):
            if reverse:
                keep = row < 8 - d
                a_sh, b_sh = pltpu.roll(a, 8 - d, 0), pltpu.roll(b, 8 - d, 0)
            else:
                keep = row >= d
                a_sh, b_sh = pltpu.roll(a, d, 0), pltpu.roll(b, d, 0)
            b = jnp.where(keep, b + a * b_sh, b)
            a = jnp.where(keep, a * a_sh, a)
        hb = b + a * h
        b_scr[pl.ds(r0, 8), :] = hb
        last = hb[0:1] if reverse else hb[7:8]
        return jnp.broadcast_to(last, (8, W))

    h_fin = lax.fori_loop(0, n_rows, block, h_scr[...], unroll=4)
    h_scr[...] = h_fin
    state_ref[0] = h_fin
    if finish:
        out_ref[0] = (hother_ref[0] + b_scr[...]) * jax.nn.gelu(gate_ref[0])
    else:
        out_ref[0] = b_scr[...]


def lru_scan(p, x_col, gate_col, h0, cw, cb, wg, bg, sp, reverse, h_other=None):
    B, n, _ = p.shape
    W = GROUP_WIDTH
    tb = _pick_tile(n, (512, 256, 128))
    nb = n // tb
    finish = h_other is not None
    tix = (lambda j: nb - 1 - j) if reverse else (lambda j: j)
    r8 = tb // 8
    in_specs = [pl.BlockSpec((1, 8, W), lambda b, j: (b, 0, 0)),
                pl.BlockSpec((1, 8, W), lambda b, j: (b, jnp.maximum(tix(j) * r8 - 1, 0), x_col)),
                pl.BlockSpec((1, tb, W), lambda b, j: (b, tix(j), x_col)),
                pl.BlockSpec((1, 8, W), lambda b, j: (b, jnp.minimum((tix(j) + 1) * r8, n // 8 - 1), x_col)),
                pl.BlockSpec((4, W), lambda b, j: (0, 0)),
                pl.BlockSpec((1, W), lambda b, j: (0, 0)),
                pl.BlockSpec((N_HEADS, W // N_HEADS, 2 * W // N_HEADS), lambda b, j: (0, 0, 0)),
                pl.BlockSpec((2, W), lambda b, j: (0, 0)),
                pl.BlockSpec((1, W), lambda b, j: (0, 0))]
    args = [h0, p, p, p, cw, cb, wg, bg, sp]
    if finish:
        in_specs += [pl.BlockSpec((1, tb, W), lambda b, j: (b, tix(j), 0)),
                     pl.BlockSpec((1, tb, W), lambda b, j: (b, tix(j), gate_col))]
        args += [h_other, p]
    kern = functools.partial(_lru_kernel, tb=tb, n_blocks=nb, reverse=reverse, finish=finish)
    return pl.pallas_call(
        kern,
        grid=(B, nb),
        in_specs=in_specs,
        out_specs=[pl.BlockSpec((1, tb, W), lambda b, j: (b, tix(j), 0)),
                   pl.BlockSpec((1, 8, W), lambda b, j: (b, 0, 0))],
        out_shape=[jax.ShapeDtypeStruct((B, n, W), F32), jax.ShapeDtypeStruct((B, 8, W), F32)],
        scratch_shapes=[pltpu.VMEM((tb, W), F32), pltpu.VMEM((tb, W), F32), pltpu.VMEM((8, W), F32)],
        compiler_params=_params(2),
        name="lru_scan",
    )(*args)


def rglru_mixer(p_ctx, p_lat, x_col, gate_col, conv_w, conv_b, wa, ba, wx, bx, lam):
    B = p_lat.shape[0]
    W = GROUP_WIDTH
    cw = conv_w.T.astype(F32)
    cb = conv_b.reshape(1, W).astype(F32)
    h0 = jnp.zeros((B, 8, W), F32)
    outs = []
    states = {}
    for d, reverse in ((0, False), (1, True)):
        wg = jnp.concatenate([wa[d], wx[d]], axis=-1).astype(BF16)
        bg = jnp.stack([ba[d], bx[d]]).astype(F32)
        sp = jax.nn.softplus(-lam[d]).reshape(1, W).astype(F32)
        states[d] = (wg, bg, sp)
    wg, bg, sp = states[0]
    hc_f, sc_f = lru_scan(p_ctx, x_col, gate_col, h0, cw, cb, wg, bg, sp, False)
    hl_f, _ = lru_scan(p_lat, x_col, gate_col, sc_f, cw, cb, wg, bg, sp, False)
    wg, bg, sp = states[1]
    out_c, sc_b = lru_scan(p_ctx, x_col, gate_col, h0, cw, cb, wg, bg, sp, True, h_other=hc_f)
    out_l, _ = lru_scan(p_lat, x_col, gate_col, sc_b, cw, cb, wg, bg, sp, True, h_other=hl_f)
    return out_c, out_l


def _split_bf16(a):
    hi = a.astype(BF16)
    return hi, (a - hi.astype(F32)).astype(BF16)


def _dot3(a, b):
    ah, al = _split_bf16(a)
    bh, bl = _split_bf16(b)
    d = functools.partial(jnp.dot, preferred_element_type=F32)
    return d(ah, bh) + (d(ah, bl) + d(al, bh))


def _dot_exact_lhs(a_exact, b):
    bh, bl = _split_bf16(b)
    d = functools.partial(jnp.dot, preferred_element_type=F32)
    return d(a_exact, bh) + d(a_exact, bl)


def _gdn_kernel(s0_ref, xprev_ref, x_ref, xnext_ref, gate_ref, cw_ref, ea_ref, dt_ref, ng_ref, *rest,
                tb, n_blocks, reverse, finish, direction):
    if finish:
        oother_ref, z_ref, out_ref, state_ref, s_scr = rest
    else:
        out_ref, state_ref, s_scr = rest
    C = DN_CHUNK
    HD = HEAD_DIM
    j = pl.program_id(1)
    t_blk = (n_blocks - 1 - j) if reverse else j

    @pl.when(j == 0)
    def _():
        s_scr[...] = s0_ref[0]

    prev = jnp.where(t_blk > 0, xprev_ref[0], 0.0)
    nxt = jnp.where(t_blk < n_blocks - 1, xnext_ref[0], 0.0)
    x_ext = jnp.concatenate([prev, x_ref[0], nxt], axis=0)
    cw = cw_ref[...]
    qkv = sum(cw[k:k + 1] * x_ext[6 + k:6 + k + tb] for k in range(4))
    qkv = qkv * jax.nn.sigmoid(qkv)

    gl_in = gate_ref[0]
    g_log = -ea_ref[...] * jax.nn.softplus(gl_in + dt_ref[...])
    beta_all = jax.nn.sigmoid(gl_in)
    r_i = lax.broadcasted_iota(jnp.int32, (tb, tb), 0)
    c_i = lax.broadcasted_iota(jnp.int32, (tb, tb), 1)
    same = (r_i // C) == (c_i // C)
    incl = same & ((c_i >= r_i) if reverse else (c_i <= r_i))
    strict = same & ((c_i > r_i) if reverse else (c_i < r_i))
    ones_where = lambda msk: jnp.where(msk, 1.0, 0.0).astype(BF16)
    g_cum = _dot_exact_lhs(ones_where(incl), g_log)
    g_tot = _dot_exact_lhs(ones_where(same), g_log)
    e_cum = jnp.exp(g_cum)
    e_rest = jnp.exp(g_tot - g_cum)
    e_tot = jnp.exp(g_tot)
    g_cum_t = jnp.transpose(g_cum)
    eye = jnp.where(r_i == c_i, 1.0, 0.0)
    dt = (((1,), (1,)), ((), ()))

    for h in range(N_HEADS):
        ld = direction * 8 + h
        lb = ld + 4
        col = lambda a, lane: jnp.broadcast_to(a[:, lane:lane + 1], (tb, HD))
        q = qkv[:, h * HD:(h + 1) * HD]
        k = qkv[:, GROUP_WIDTH + h * HD:GROUP_WIDTH + (h + 1) * HD]
        v = qkv[:, 2 * GROUP_WIDTH + h * HD:2 * GROUP_WIDTH + (h + 1) * HD]
        q = q * lax.rsqrt(jnp.sum(q * q, axis=-1, keepdims=True) + 1e-6) * HD ** -0.5
        k = k * lax.rsqrt(jnp.sum(k * k, axis=-1, keepdims=True) + 1e-6)
        beta = col(beta_all, lb)
        kb = k * beta
        k16 = k.astype(BF16)
        decay = jnp.exp(jnp.where(incl, col(g_cum, ld)[:, :tb] - g_cum_t[ld:ld + 1, :], -jnp.inf))
        m_low = jnp.where(strict, lax.dot_general(kb.astype(BF16), k16, dt, preferred_element_type=F32) * decay, 0.0)
        attn = lax.dot_general(q.astype(BF16), k16, dt, preferred_element_type=F32) * decay
        y = -m_low
        t_inv = eye + y
        for _ in range(5):
            y = _dot3(y, y)
            t_inv = t_inv + _dot3(t_inv, y)
        t16 = t_inv.astype(BF16)
        rhs = jnp.concatenate([v * beta, kb * col(e_cum, ld)], axis=1).astype(BF16)
        uw = jnp.dot(t16, rhs, preferred_element_type=F32)
        u, w = uw[:, :HD], uw[:, HD:]
        k_dec = (k * col(e_rest, ld)).astype(BF16)
        q_dec = (q * col(e_cum, ld)).astype(BF16)
        s = s_scr[h]
        n_ch = tb // C
        v_new = [None] * n_ch
        o_inter = [None] * n_ch
        for ci in (range(n_ch - 1, -1, -1) if reverse else range(n_ch)):
            rows = slice(ci * C, (ci + 1) * C)
            s16 = s.astype(BF16)
            v_new[ci] = u[rows] - jnp.dot(w[rows].astype(BF16), s16, preferred_element_type=F32)
            o_inter[ci] = jnp.dot(q_dec[rows], s16, preferred_element_type=F32)
            s = s * col(e_tot, ld)[ci * C:ci * C + 1] + lax.dot_general(
                k_dec[rows], v_new[ci].astype(BF16), (((0,), (0,)), ((), ())), preferred_element_type=F32)
        s_scr[h] = s
        v_all = jnp.concatenate(v_new, axis=0).astype(BF16)
        o = jnp.concatenate(o_inter, axis=0) + jnp.dot(attn.astype(BF16), v_all, preferred_element_type=F32)
        if finish:
            o = o + oother_ref[0, :, h * HD:(h + 1) * HD]
            o = o * lax.rsqrt(jnp.mean(o * o, axis=-1, keepdims=True) + 1e-6) * ng_ref[...]
            z = z_ref[0, :, h * HD:(h + 1) * HD]
            o = o * (z * jax.nn.sigmoid(z))
        out_ref[0, :, h * HD:(h + 1) * HD] = o
    state_ref[0] = s_scr[...]


def gdn_scan(p, pg, s0, cw, ea, dt, ng, direction, o_other=None):
    B, n, _ = p.shape
    G = GROUP_WIDTH
    reverse = direction == 1
    tb = _pick_tile(n, (128,))
    nb = n // tb
    finish = o_other is not None
    tix = (lambda j: nb - 1 - j) if reverse else (lambda j: j)
    r8 = tb // 8
    vec = lambda w: pl.BlockSpec((1, w), lambda b, j: (0, 0))
    in_specs = [pl.BlockSpec((1, N_HEADS, HEAD_DIM, HEAD_DIM), lambda b, j: (b, 0, 0, 0)),
                pl.BlockSpec((1, 8, 3 * G), lambda b, j: (b, jnp.maximum(tix(j) * r8 - 1, 0), 0)),
                pl.BlockSpec((1, tb, 3 * G), lambda b, j: (b, tix(j), 0)),
                pl.BlockSpec((1, 8, 3 * G), lambda b, j: (b, jnp.minimum((tix(j) + 1) * r8, n // 8 - 1), 0)),
                pl.BlockSpec((1, tb, 128), lambda b, j: (b, tix(j), 0)),
                pl.BlockSpec((4, 3 * G), lambda b, j: (0, 0)),
                vec(128), vec(128), vec(HEAD_DIM)]
    args = [s0, p, p, p, pg, cw, ea, dt, ng]
    if finish:
        in_specs += [pl.BlockSpec((1, tb, G), lambda b, j: (b, tix(j), 0)),
                     pl.BlockSpec((1, tb, G), lambda b, j: (b, tix(j), 3))]
        args += [o_other, p]
    kern = functools.partial(_gdn_kernel, tb=tb, n_blocks=nb, reverse=reverse, finish=finish, direction=direction)
    return pl.pallas_call(
        kern,
        grid=(B, nb),
        in_specs=in_specs,
        out_specs=[pl.BlockSpec((1, tb, G), lambda b, j: (b, tix(j), 0)),
                   pl.BlockSpec((1, N_HEADS, HEAD_DIM, HEAD_DIM), lambda b, j: (b, 0, 0, 0))],
        out_shape=[jax.ShapeDtypeStruct((B, n, G), F32),
                   jax.ShapeDtypeStruct((B, N_HEADS, HEAD_DIM, HEAD_DIM), F32)],
        scratch_shapes=[pltpu.VMEM((N_HEADS, HEAD_DIM, HEAD_DIM), F32)],
        compiler_params=_params(2),
        name="gdn_scan",
    )(*args)


def gdn_mixer(p_ctx, g_ctx, p_lat, g_lat, conv_w, a_log, dt_bias, norm_g):
    B = p_lat.shape[0]
    cw = conv_w.T.astype(F32)
    zeros4 = jnp.zeros((2, N_HEADS), F32)
    lanes = lambda t: jnp.pad(jnp.stack([t, zeros4], axis=1).reshape(1, -1), ((0, 0), (0, 128 - 4 * N_HEADS)))
    ea = lanes(jnp.exp(a_log.astype(F32)))
    dt = lanes(dt_bias.astype(F32))
    ng = norm_g.reshape(1, HEAD_DIM).astype(F32)
    s0 = jnp.zeros((B, N_HEADS, HEAD_DIM, HEAD_DIM), F32)
    oc_f, sc_f = gdn_scan(p_ctx, g_ctx, s0, cw, ea, dt, ng, 0)
    ol_f, _ = gdn_scan(p_lat, g_lat, sc_f, cw, ea, dt, ng, 0)
    out_c, sc_b = gdn_scan(p_ctx, g_ctx, s0, cw, ea, dt, ng, 1, o_other=oc_f)
    out_l, _ = gdn_scan(p_lat, g_lat, sc_b, cw, ea, dt, ng, 1, o_other=ol_f)
    return out_c, out_l


def _moe_kernel(be_ref, nused_ref, x_ref, w1_ref, w3_ref, w2_ref, wr_ref, y_ref):
    i = pl.program_id(0)

    @pl.when(i < nused_ref[0])
    def _():
        x = x_ref[...]
        h1 = jnp.dot(x, w1_ref[0], preferred_element_type=F32)
        h3 = jnp.dot(x, w3_ref[0], preferred_element_type=F32)
        hid = (h1 * jax.nn.sigmoid(h1) * h3).astype(BF16)
        y_ref[...] = jnp.dot(hid, w2_ref[0], preferred_element_type=F32) * wr_ref[...]

    @pl.when(i >= nused_ref[0])
    def _():
        y_ref[...] = jnp.zeros(y_ref.shape, y_ref.dtype)


def grouped_experts(x_sorted, blk_e, n_used, wr, w1, w3, w2, tm):
    R, D = x_sorted.shape
    FF = w1.shape[-1]
    grid_spec = pltpu.PrefetchScalarGridSpec(
        num_scalar_prefetch=2,
        grid=(R // tm,),
        in_specs=[pl.BlockSpec((tm, D), lambda i, be, nu: (i, 0)),
                  pl.BlockSpec((1, D, FF), lambda i, be, nu: (be[i], 0, 0)),
                  pl.BlockSpec((1, D, FF), lambda i, be, nu: (be[i], 0, 0)),
                  pl.BlockSpec((1, FF, D), lambda i, be, nu: (be[i], 0, 0)),
                  pl.BlockSpec((tm, 1), lambda i, be, nu: (i, 0))],
        out_specs=pl.BlockSpec((tm, D), lambda i, be, nu: (i, 0)),
    )
    return pl.pallas_call(
        _moe_kernel,
        grid_spec=grid_spec,
        out_shape=jax.ShapeDtypeStruct((R, D), F32),
        compiler_params=_params(1),
        name="grouped_experts",
    )(blk_e, n_used, x_sorted, w1, w3, w2, wr)


def _shared_kernel(x_ref, w1_ref, w3_ref, w2_ref, y_ref):
    x = x_ref[...].astype(BF16)
    h1 = jnp.dot(x, w1_ref[...], preferred_element_type=F32)
    h3 = jnp.dot(x, w3_ref[...], preferred_element_type=F32)
    hid = (h1 * jax.nn.sigmoid(h1) * h3).astype(BF16)
    y_ref[...] = jnp.dot(hid, w2_ref[...], preferred_element_type=F32)


def shared_expert(tok, w1, w3, w2):
    T, D = tok.shape
    FF = w1.shape[-1]
    tm = _pick_tile(T, (512, 256, 128))
    return pl.pallas_call(
        _shared_kernel,
        grid=(T // tm,),
        in_specs=[pl.BlockSpec((tm, D), lambda i: (i, 0)),
                  pl.BlockSpec((D, FF), lambda i: (0, 0)),
                  pl.BlockSpec((D, FF), lambda i: (0, 0)),
                  pl.BlockSpec((FF, D), lambda i: (0, 0))],
        out_specs=pl.BlockSpec((tm, D), lambda i: (i, 0)),
        out_shape=jax.ShapeDtypeStruct((T, D), F32),
        compiler_params=_params(1),
        name="shared_expert",
    )(tok, w1, w3, w2)


def moe_ffn(tok, router_w, router_bias, w1, w3, w2, sw1, sw3, sw2):
    T, D = tok.shape
    tm = 512 if T * TOP_K >= 64 * 512 else 128
    scores = jax.nn.sigmoid(jnp.matmul(tok, router_w, preferred_element_type=F32))
    _, idx = lax.top_k(scores + router_bias.astype(F32), TOP_K)
    wts = jnp.take_along_axis(scores, idx, -1)
    wts = wts / jnp.sum(wts, -1, keepdims=True) * ROUTED_SCALE
    A = T * TOP_K
    flat_e = idx.reshape(-1).astype(jnp.int32)
    aid = jnp.arange(A, dtype=jnp.int32)
    _, order, w_sorted = lax.sort((flat_e * A + aid, aid, wts.reshape(-1)), num_keys=1)
    counts = jnp.sum((flat_e[:, None] == jnp.arange(N_EXPERTS, dtype=jnp.int32)[None, :]).astype(jnp.int32), axis=0)
    ends = jnp.cumsum(counts)
    starts = ends - counts
    padded = (counts + tm - 1) // tm * tm
    pad_ends = jnp.cumsum(padded)
    pad_starts = pad_ends - padded
    gap_before = pad_starts - starts
    gap_step = jnp.diff(gap_before, prepend=0)
    shift = jnp.sum(jnp.where(aid[:, None] >= starts[None, :], gap_step[None, :], 0), axis=1)
    _, dest = lax.sort((order, aid + shift), num_keys=1)
    n_blocks = -(-A // tm) + N_EXPERTS
    R = n_blocks * tm
    blk_e = jnp.minimum(jnp.sum((jnp.arange(n_blocks, dtype=jnp.int32)[:, None] * tm >= pad_ends[None, :])
                                .astype(jnp.int32), axis=1), N_EXPERTS - 1)
    n_used = (pad_ends[-1] // tm).astype(jnp.int32).reshape(1)
    row = jnp.arange(R, dtype=jnp.int32).reshape(n_blocks, tm)
    src = row - gap_before[blk_e][:, None]
    valid = (row < (pad_starts + counts)[blk_e][:, None]) & (row < pad_ends[-1])
    src = jnp.where(valid, src, 0).reshape(R)
    buf_tok = jnp.where(valid.reshape(R), order[src] // TOP_K, 0)
    buf_w = jnp.where(valid.reshape(R), w_sorted[src], 0.0)
    x_sorted = tok.astype(BF16)[buf_tok]
    y_sorted = grouped_experts(x_sorted, blk_e, n_used, buf_w.reshape(R, 1), w1, w3, w2, tm)
    routed = jnp.sum(y_sorted[dest.reshape(T, TOP_K).T], axis=0)
    return routed + shared_expert(tok, sw1, sw3, sw2)


def layer_norm(t, g, b, eps=1e-5):
    mu = jnp.mean(t, -1, keepdims=True)
    var = jnp.mean(jnp.square(t - mu), -1, keepdims=True)
    return (t - mu) * lax.rsqrt(var + eps) * g + b


def rms_norm(t, g, eps=1e-6):
    return t * lax.rsqrt(jnp.mean(jnp.square(t), -1, keepdims=True) + eps) * g


def l2norm(t, eps=1e-6):
    return t * lax.rsqrt(jnp.sum(jnp.square(t), -1, keepdims=True) + eps)


def depthwise_conv(t, w, pad_left, pad_right):
    return lax.conv_general_dilated(t, w.T[:, None, :].astype(t.dtype), window_strides=(1,),
                                    padding=[(pad_left, pad_right)],
                                    dimension_numbers=('NWC', 'WIO', 'NWC'),
                                    feature_group_count=t.shape[-1])


def axial_rope(n_lat, dim):
    rows = n_lat // GRID_W
    n_freq = dim // 4
    inv = ROPE_BASE ** (-jnp.arange(n_freq, dtype=F32) / n_freq)
    row_pos = jnp.repeat(jnp.arange(rows, dtype=F32), GRID_W)
    col_pos = (jnp.arange(n_lat) % GRID_W).astype(F32)
    ang = jnp.concatenate([row_pos[:, None] * inv, col_pos[:, None] * inv], -1)
    return jnp.cos(ang), jnp.sin(ang)


def apply_rope(t, cos, sin):
    tf = t.reshape(*t.shape[:-1], -1, 2)
    t1, t2 = tf[..., 0], tf[..., 1]
    out = jnp.stack([t1 * cos - t2 * sin, t1 * sin + t2 * cos], -1)
    return out.reshape(t.shape)


def diff_attn_mixer(p_ctx, p_lat, rope_cos, rope_sin, lam_vecs, norm_g, layer_idx, with_ctx):
    G = GROUP_WIDTH

    def split_heads(p):
        B, n, _ = p.shape
        q = p[..., :G].reshape(B, n, N_HEADS, 2, DA_QK)
        k = p[..., G:2 * G].reshape(B, n, N_HEADS, 2, DA_QK)
        return q, k, p[..., 2 * G:]

    qc, kc, vc = split_heads(p_ctx)
    ql, kl, vl = split_heads(p_lat)
    B, n = ql.shape[:2]
    L = qc.shape[1]
    cos = rope_cos[None, :, None, None, :]
    sin = rope_sin[None, :, None, None, :]
    scale = DA_QK ** -0.5
    ql = (apply_rope(ql, cos, sin) * scale).reshape(B, n, G).astype(BF16)
    kl = apply_rope(kl, cos, sin).reshape(B, n, G)
    kc = kc.reshape(B, L, G)
    lam_init = 0.8 - 0.6 * math.exp(-0.3 * layer_idx)
    lv = lam_vecs.astype(F32)
    lam = jnp.exp(jnp.sum(lv[0] * lv[1])) - jnp.exp(jnp.sum(lv[2] * lv[3])) + lam_init
    k_all = jnp.concatenate([kc, kl], 1).astype(BF16)
    v_all = jnp.concatenate([vc, vl], 1).astype(BF16)
    out_l = diff_attention(ql, jnp.swapaxes(k_all, 1, 2), v_all, lam, norm_g, 1.0 - lam_init)
    out_c = None
    if with_ctx:
        out_c = diff_attention((qc * scale).reshape(B, L, G).astype(BF16), jnp.swapaxes(kc.astype(BF16), 1, 2),
                               vc.astype(BF16), lam, norm_g, 1.0 - lam_init)
    return out_c, out_l


def short_conv_mixer(p, conv_w):
    b_g, c_g, xs = jnp.split(p, 3, -1)
    return b_g * depthwise_conv(c_g * xs, conv_w, 1, 1)


def kernel(x, c, ctx, c_ctx, w_ada, b_ada, w_in, dn_conv, dn_a_log, dn_dt_bias, dn_norm, lru_conv, lru_conv_b, lru_wa, lru_ba, lru_wx, lru_bx, lru_lambda, da_lambda, da_norm, sc_conv, w_out, ln1_g, ln1_b, router_w, router_bias, exp_w1, exp_w3, exp_w2, sh_w1, sh_w3, sh_w2, ln2_g, ln2_b):
    depth = w_in.shape[0]
    alpha = (2 * depth) ** 0.25
    B, N, D = x.shape
    L = ctx.shape[1]
    G = GROUP_WIDTH
    silu_c = jax.nn.silu(c)[:, None, :]
    silu_cc = jax.nn.silu(c_ctx)[None, None, :]
    rope_cos, rope_sin = axial_rope(N, DA_QK)
    xc = ctx
    n_gate = 4 * N_HEADS
    for l in range(depth):
        with_ctx = l < depth - 1
        sh1, s1, g1, sh2, s2, g2 = jnp.split(silu_c @ w_ada[l] + b_ada[l], 6, -1)
        sh1c, s1c, g1c, sh2c, s2c, g2c = jnp.split(silu_cc @ w_ada[l] + b_ada[l], 6, -1)
        w_l = w_in[l]
        w_main = jnp.concatenate([w_l[:, :4 * G], w_l[:, 4 * G + n_gate:]], axis=1).astype(BF16)
        w_gate = jnp.pad(w_l[:, 4 * G:4 * G + n_gate], ((0, 0), (0, 128 - n_gate))).astype(BF16)
        a_lat = (x * (1 + s1) + sh1).reshape(B * N, D)
        a_ctx = jnp.broadcast_to(xc * (1 + s1c) + sh1c, (B, L, D)).reshape(B * L, D)
        pl_all = dense_matmul(a_lat, w_main).reshape(B, N, -1)
        pc_all = dense_matmul(a_ctx, w_main).reshape(B, L, -1)
        gl_all = dense_matmul(a_lat, w_gate).reshape(B, N, -1)
        gc_all = dense_matmul(a_ctx, w_gate).reshape(B, L, -1)
        a_c, a_l = gdn_mixer(pc_all, gc_all, pl_all, gl_all, dn_conv[l], dn_a_log[l], dn_dt_bias[l], dn_norm[l])
        b_c, b_l = rglru_mixer(pc_all, pl_all, 4, 5, lru_conv[l], lru_conv_b[l], lru_wa[l], lru_ba[l],
                               lru_wx[l], lru_bx[l], lru_lambda[l])
        c_c, c_l = diff_attn_mixer(pc_all[..., 6 * G:9 * G], pl_all[..., 6 * G:9 * G], rope_cos, rope_sin,
                                   da_lambda[l], da_norm[l], l, with_ctx)
        p_lat = (None, None, None, pl_all[..., 9 * G:12 * G])
        p_ctx = (None, None, None, pc_all[..., 9 * G:12 * G])
        d_l = short_conv_mixer(p_lat[3], sc_conv[l])
        w_out_l = w_out[l].astype(BF16)
        mix_l = dense_matmul(jnp.concatenate([a_l, b_l, c_l, d_l], -1).reshape(B * N, D), w_out_l).reshape(B, N, D)
        x_mid = layer_norm(alpha * x + g1 * mix_l, ln1_g[l], ln1_b[l])
        moe_args = (router_w[l], router_bias[l], exp_w1[l].astype(BF16), exp_w3[l].astype(BF16),
                    exp_w2[l].astype(BF16), sh_w1[l].astype(BF16), sh_w3[l].astype(BF16), sh_w2[l].astype(BF16))
        if with_ctx:
            d_c = short_conv_mixer(p_ctx[3], sc_conv[l])
            mix_c = dense_matmul(jnp.concatenate([a_c, b_c, c_c, d_c], -1).reshape(B * L, D), w_out_l).reshape(B, L, D)
            xc_mid = layer_norm(alpha * xc + g1c * mix_c, ln1_g[l], ln1_b[l])
            tok = jnp.concatenate([(xc_mid * (1 + s2c) + sh2c).reshape(-1, D),
                                   (x_mid * (1 + s2) + sh2).reshape(-1, D)], 0)
            y = moe_ffn(tok, *moe_args)
            xc = layer_norm(alpha * xc_mid + g2c * y[:B * L].reshape(B, L, D), ln2_g[l], ln2_b[l])
            y_l = y[B * L:].reshape(B, N, D)
        else:
            y_l = moe_ffn((x_mid * (1 + s2) + sh2).reshape(-1, D), *moe_args).reshape(B, N, D)
        x = layer_norm(alpha * x_mid + g2 * y_l, ln2_g[l], ln2_b[l])
    return x
```

```python
import functools
import math

import jax
import jax.numpy as jnp
from jax import lax
from jax.experimental import pallas as pl
from jax.experimental.pallas import tpu as pltpu

F32 = jnp.float32
BF16 = jnp.bfloat16

GRID_W = 64
GROUP_WIDTH = 512
HEAD_DIM = 128
N_HEADS = GROUP_WIDTH // HEAD_DIM
DN_CHUNK = 64
LRU_C = 8.0
DA_QK = HEAD_DIM // 2
ROPE_BASE = 10000.0
N_EXPERTS = 64
TOP_K = 6
ROUTED_SCALE = 2.5

ATTN_TQ = (1024, 512, 256, 128)
ATTN_TK = (640, 512, 256, 128)
ATTN_UNROLL = 2

V7X_VMEM_BYTES = 64 * 1024 * 1024
VMEM_LIMIT = V7X_VMEM_BYTES * 3 // 4


def _params(n_axes):
    return pltpu.CompilerParams(dimension_semantics=("arbitrary",) * n_axes, vmem_limit_bytes=VMEM_LIMIT)


def _pick_tile(n, candidates):
    for c in candidates:
        if n % c == 0:
            return c
    return n


def _mod_mm_kernel(a_ref, sc_ref, sh_ref, w_ref, o_ref, a_scr):
    @pl.when(pl.program_id(1) == 0)
    def _():
        a_scr[...] = (a_ref[...] * (1.0 + sc_ref[0]) + sh_ref[0]).astype(BF16)

    o_ref[...] = jnp.dot(a_scr[...], w_ref[...], preferred_element_type=F32)


def modulated_matmul(a, scale, shift, w, rows_per_group):
    M, K = a.shape
    N = w.shape[1]
    tm = _pick_tile(rows_per_group, (1024, 512, 256, 128))
    tn = _pick_tile(N, (512, 256, 128))
    tiles_per_group = rows_per_group // tm
    grp = lambda i, j: (i // tiles_per_group, 0, 0)
    return pl.pallas_call(
        _mod_mm_kernel,
        grid=(M // tm, N // tn),
        in_specs=[pl.BlockSpec((tm, K), lambda i, j: (i, 0)),
                  pl.BlockSpec((1, 1, K), grp),
                  pl.BlockSpec((1, 1, K), grp),
                  pl.BlockSpec((K, tn), lambda i, j: (0, j))],
        out_specs=pl.BlockSpec((tm, tn), lambda i, j: (i, j)),
        out_shape=jax.ShapeDtypeStruct((M, N), F32),
        scratch_shapes=[pltpu.VMEM((tm, K), BF16)],
        compiler_params=_params(2),
        name="modulated_matmul",
    )(a, scale, shift, w)


def _layer_norm_rows(r, g, b, eps):
    mu = jnp.mean(r, axis=-1, keepdims=True)
    d = r - mu
    return d * lax.rsqrt(jnp.mean(d * d, axis=-1, keepdims=True) + eps) * g + b


def _out_proj_kernel(a_ref, b_ref, c_ref, d_ref, w_ref, x_ref, g1_ref, s2_ref, sh2_ref, lg_ref, lb_ref,
                     xmid_ref, tok_ref, *, alpha):
    G = GROUP_WIDTH
    mix = sum(jnp.dot(r[...].astype(BF16), w_ref[g * G:(g + 1) * G, :], preferred_element_type=F32)
              for g, r in enumerate((a_ref, b_ref, c_ref, d_ref)))
    x_mid = _layer_norm_rows(alpha * x_ref[...] + g1_ref[0] * mix, lg_ref[...], lb_ref[...], 1e-5)
    xmid_ref[...] = x_mid
    tok_ref[...] = x_mid * (1.0 + s2_ref[0]) + sh2_ref[0]


def out_proj_norm(mix_parts, w, x, g1, s2, sh2, ln_g, ln_b, alpha, rows_per_group):
    M, D = x.shape
    G = GROUP_WIDTH
    tm = _pick_tile(rows_per_group, (256, 128))
    tiles_per_group = rows_per_group // tm
    grp = lambda i: (i // tiles_per_group, 0, 0)
    row = lambda width: pl.BlockSpec((tm, width), lambda i: (i, 0))
    vec = pl.BlockSpec((1, D), lambda i: (0, 0))
    return pl.pallas_call(
        functools.partial(_out_proj_kernel, alpha=alpha),
        grid=(M // tm,),
        in_specs=[row(G), row(G), row(G), row(G), pl.BlockSpec((D, D), lambda i: (0, 0)), row(D),
                  pl.BlockSpec((1, 1, D), grp), pl.BlockSpec((1, 1, D), grp), pl.BlockSpec((1, 1, D), grp), vec, vec],
        out_specs=[row(D), row(D)],
        out_shape=[jax.ShapeDtypeStruct((M, D), F32), jax.ShapeDtypeStruct((M, D), F32)],
        compiler_params=_params(1),
        name="out_proj_norm",
    )(*mix_parts, w, x, g1, s2, sh2, ln_g.reshape(1, D), ln_b.reshape(1, D))


def _attn_kernel(lam_ref, g_ref, q_ref, kt_ref, v_ref, o_ref, m_scr, acc_scr, *, tk, n_chunks, out_scale):
    q = q_ref[0]
    lane = lax.broadcasted_iota(jnp.int32, q.shape, 1)
    zero = jnp.zeros_like(q)
    q_maps = (jnp.where(lane < DA_QK, q, zero), jnp.where(lane >= DA_QK, q, zero))
    m_scr[...] = jnp.full(m_scr.shape, -jnp.inf, F32)
    acc_scr[...] = jnp.zeros(acc_scr.shape, F32)
    ones = jnp.ones((tk, HEAD_DIM), BF16)

    def chunk(c, carry):
        off = pl.multiple_of(c * tk, tk)
        ktc = kt_ref[0, :, pl.ds(off, tk)]
        v_ext = jnp.concatenate([v_ref[0, pl.ds(off, tk), :], ones], axis=1)
        for mp in range(2):
            s = jnp.dot(q_maps[mp], ktc, preferred_element_type=F32)
            m_prev = m_scr[mp]
            m_new = jnp.maximum(m_prev, jnp.max(s, axis=-1, keepdims=True))
            p = jnp.exp(s - pltpu.repeat(m_new, tk // HEAD_DIM, axis=1))
            alpha = jnp.exp(m_prev - m_new)
            acc_scr[mp] = (pltpu.repeat(alpha, 2, axis=1) * acc_scr[mp]
                           + jnp.dot(p.astype(BF16), v_ext, preferred_element_type=F32))
            m_scr[mp] = m_new
        return carry

    lax.fori_loop(0, n_chunks, chunk, 0, unroll=ATTN_UNROLL)
    a0, a1 = acc_scr[0], acc_scr[1]
    o = a0[:, :HEAD_DIM] / a0[:, HEAD_DIM:] - lam_ref[...] * (a1[:, :HEAD_DIM] / a1[:, HEAD_DIM:])
    o = o * lax.rsqrt(jnp.mean(o * o, axis=-1, keepdims=True) + 1e-6)
    o_ref[0] = o * g_ref[...] * out_scale


def diff_attention(q, kt, v, lam, norm_g, out_scale):
    B, n, G = q.shape
    m = v.shape[1]
    tq = _pick_tile(n, ATTN_TQ)
    tk = _pick_tile(m, ATTN_TK)
    kern = functools.partial(_attn_kernel, tk=tk, n_chunks=m // tk, out_scale=out_scale)
    lam_row = jnp.full((1, HEAD_DIM), lam, F32)
    return pl.pallas_call(
        kern,
        grid=(B, N_HEADS, n // tq),
        in_specs=[pl.BlockSpec((1, HEAD_DIM), lambda b, h, i: (0, 0)),
                  pl.BlockSpec((1, HEAD_DIM), lambda b, h, i: (0, 0)),
                  pl.BlockSpec((1, tq, HEAD_DIM), lambda b, h, i: (b, i, h)),
                  pl.BlockSpec((1, HEAD_DIM, m), lambda b, h, i: (b, h, 0)),
                  pl.BlockSpec((1, m, HEAD_DIM), lambda b, h, i: (b, 0, h))],
        out_specs=pl.BlockSpec((1, tq, HEAD_DIM), lambda b, h, i: (b, i, h)),
        out_shape=jax.ShapeDtypeStruct((B, n, G), F32),
        scratch_shapes=[pltpu.VMEM((2, tq, HEAD_DIM), F32), pltpu.VMEM((2, tq, 2 * HEAD_DIM), F32)],
        compiler_params=_params(3),
        name="diff_attention",
    )(lam_row, norm_g.reshape(1, HEAD_DIM).astype(F32), q, kt, v)


def _lru_kernel(h0_ref, xprev_ref, x_ref, xnext_ref, cw_ref, cb_ref, wg_ref, bg_ref, sp_ref, *rest,
                tb, n_blocks, reverse, finish):
    if finish:
        hother_ref, gate_ref, out_ref, state_ref, a_scr, b_scr, h_scr = rest
    else:
        out_ref, state_ref, a_scr, b_scr, h_scr = rest
    j = pl.program_id(1)
    t_blk = (n_blocks - 1 - j) if reverse else j
    W = x_ref.shape[-1]

    @pl.when(j == 0)
    def _():
        h_scr[...] = jnp.broadcast_to(h0_ref[0], h_scr.shape)

    prev = jnp.where(t_blk > 0, xprev_ref[0], 0.0)
    nxt = jnp.where(t_blk < n_blocks - 1, xnext_ref[0], 0.0)
    x_ext = jnp.concatenate([prev, x_ref[0], nxt], axis=0)
    cw = cw_ref[...]
    xb = cb_ref[...] + sum(cw[k:k + 1] * x_ext[6 + k:6 + k + tb] for k in range(4))
    xb16 = xb.astype(BF16)
    hw = W // N_HEADS
    pre = [jnp.dot(xb16[:, h * hw:(h + 1) * hw], wg_ref[h], preferred_element_type=F32) for h in range(N_HEADS)]
    pre_r = jnp.concatenate([p[:, :hw] for p in pre], axis=1) + bg_ref[0:1]
    pre_i = jnp.concatenate([p[:, hw:] for p in pre], axis=1) + bg_ref[1:2]
    log_a = -LRU_C * jax.nn.sigmoid(pre_r) * sp_ref[...]
    a = jnp.exp(log_a)
    a_scr[...] = a
    b_scr[...] = jnp.sqrt(-jnp.tanh(log_a) * (a * a + 1.0)) * (jax.nn.sigmoid(pre_i) * xb)

    row = lax.broadcasted_iota(jnp.int32, (8, W), 0)
    n_rows = tb // 8

    def block(i, h):
        blk = (n_rows - 1 - i) if reverse else i
        r0 = pl.multiple_of(blk * 8, 8)
        a = a_scr[pl.ds(r0, 8), :]
        b = b_scr[pl.ds(r0, 8), :]
        for d in (1, 2, 4):
            if reverse:
                keep = row < 8 - d
                a_sh, b_sh = pltpu.roll(a, 8 - d, 0), pltpu.roll(b, 8 - d, 0)
            else:
                keep = row >= d
                a_sh, b_sh = pltpu.roll(a, d, 0), pltpu.roll(b, d, 0)
            b = jnp.where(keep, b + a * b_sh, b)
            a = jnp.where(keep, a * a_sh, a)
        hb = b + a * h
        b_scr[pl.ds(r0, 8), :] = hb
        last = hb[0:1] if reverse else hb[7:8]
        return jnp.broadcast_to(last, (8, W))

    h_fin = lax.fori_loop(0, n_rows, block, h_scr[...], unroll=4)
    h_scr[...] = h_fin
    state_ref[0] = h_fin
    if finish:
        out_ref[0] = (hother_ref[0] + b_scr[...]) * jax.nn.gelu(gate_ref[0])
    else:
        out_ref[0] = b_scr[...]


def lru_scan(p, x_col, gate_col, h0, cw, cb, wg, bg, sp, reverse, h_other=None):
    B, n, _ = p.shape
    W = GROUP_WIDTH
    tb = _pick_tile(n, (512, 256, 128))
    nb = n // tb
    finish = h_other is not None
    tix = (lambda j: nb - 1 - j) if reverse else (lambda j: j)
    r8 = tb // 8
    in_specs = [pl.BlockSpec((1, 8, W), lambda b, j: (b, 0, 0)),
                pl.BlockSpec((1, 8, W), lambda b, j: (b, jnp.maximum(tix(j) * r8 - 1, 0), x_col)),
                pl.BlockSpec((1, tb, W), lambda b, j: (b, tix(j), x_col)),
                pl.BlockSpec((1, 8, W), lambda b, j: (b, jnp.minimum((tix(j) + 1) * r8, n // 8 - 1), x_col)),
                pl.BlockSpec((4, W), lambda b, j: (0, 0)),
                pl.BlockSpec((1, W), lambda b, j: (0, 0)),
                pl.BlockSpec((N_HEADS, W // N_HEADS, 2 * W // N_HEADS), lambda b, j: (0, 0, 0)),
                pl.BlockSpec((2, W), lambda b, j: (0, 0)),
                pl.BlockSpec((1, W), lambda b, j: (0, 0))]
    args = [h0, p, p, p, cw, cb, wg, bg, sp]
    if finish:
        in_specs += [pl.BlockSpec((1, tb, W), lambda b, j: (b, tix(j), 0)),
                     pl.BlockSpec((1, tb, W), lambda b, j: (b, tix(j), gate_col))]
        args += [h_other, p]
    kern = functools.partial(_lru_kernel, tb=tb, n_blocks=nb, reverse=reverse, finish=finish)
    return pl.pallas_call(
        kern,
        grid=(B, nb),
        in_specs=in_specs,
        out_specs=[pl.BlockSpec((1, tb, W), lambda b, j: (b, tix(j), 0)),
                   pl.BlockSpec((1, 8, W), lambda b, j: (b, 0, 0))],
        out_shape=[jax.ShapeDtypeStruct((B, n, W), F32), jax.ShapeDtypeStruct((B, 8, W), F32)],
        scratch_shapes=[pltpu.VMEM((tb, W), F32), pltpu.VMEM((tb, W), F32), pltpu.VMEM((8, W), F32)],
        compiler_params=_params(2),
        name="lru_scan",
    )(*args)


def rglru_mixer(p_ctx, p_lat, x_col, gate_col, conv_w, conv_b, wa, ba, wx, bx, lam):
    B = p_lat.shape[0]
    W = GROUP_WIDTH
    cw = conv_w.T.astype(F32)
    cb = conv_b.reshape(1, W).astype(F32)
    h0 = jnp.zeros((B, 8, W), F32)
    outs = []
    states = {}
    for d, reverse in ((0, False), (1, True)):
        wg = jnp.concatenate([wa[d], wx[d]], axis=-1).astype(BF16)
        bg = jnp.stack([ba[d], bx[d]]).astype(F32)
        sp = jax.nn.softplus(-lam[d]).reshape(1, W).astype(F32)
        states[d] = (wg, bg, sp)
    wg, bg, sp = states[0]
    hc_f, sc_f = lru_scan(p_ctx, x_col, gate_col, h0, cw, cb, wg, bg, sp, False)
    hl_f, _ = lru_scan(p_lat, x_col, gate_col, sc_f, cw, cb, wg, bg, sp, False)
    wg, bg, sp = states[1]
    out_c, sc_b = lru_scan(p_ctx, x_col, gate_col, h0, cw, cb, wg, bg, sp, True, h_other=hc_f)
    out_l, _ = lru_scan(p_lat, x_col, gate_col, sc_b, cw, cb, wg, bg, sp, True, h_other=hl_f)
    return out_c, out_l


def _split_bf16(a):
    hi = a.astype(BF16)
    return hi, (a - hi.astype(F32)).astype(BF16)


def _dot3(a, b):
    ah, al = _split_bf16(a)
    bh, bl = _split_bf16(b)
    d = functools.partial(jnp.dot, preferred_element_type=F32)
    return d(ah, bh) + (d(ah, bl) + d(al, bh))


def _dot_exact_lhs(a_exact, b):
    bh, bl = _split_bf16(b)
    d = functools.partial(jnp.dot, preferred_element_type=F32)
    return d(a_exact, bh) + d(a_exact, bl)


def _gdn_kernel(s0_ref, xprev_ref, x_ref, xnext_ref, gate_ref, cw_ref, ea_ref, dt_ref, ng_ref, *rest,
                tb, n_blocks, reverse, finish, direction):
    if finish:
        oother_ref, z_ref, out_ref, state_ref, s_scr = rest
    else:
        out_ref, state_ref, s_scr = rest
    C = DN_CHUNK
    HD = HEAD_DIM
    j = pl.program_id(1)
    t_blk = (n_blocks - 1 - j) if reverse else j

    @pl.when(j == 0)
    def _():
        s_scr[...] = s0_ref[0]

    prev = jnp.where(t_blk > 0, xprev_ref[0], 0.0)
    nxt = jnp.where(t_blk < n_blocks - 1, xnext_ref[0], 0.0)
    x_ext = jnp.concatenate([prev, x_ref[0], nxt], axis=0)
    cw = cw_ref[...]
    qkv = sum(cw[k:k + 1] * x_ext[6 + k:6 + k + tb] for k in range(4))
    qkv = qkv * jax.nn.sigmoid(qkv)

    gl_in = gate_ref[0]
    g_log = -ea_ref[...] * jax.nn.softplus(gl_in + dt_ref[...])
    beta_all = jax.nn.sigmoid(gl_in)
    r_i = lax.broadcasted_iota(jnp.int32, (tb, tb), 0)
    c_i = lax.broadcasted_iota(jnp.int32, (tb, tb), 1)
    same = (r_i // C) == (c_i // C)
    incl = same & ((c_i >= r_i) if reverse else (c_i <= r_i))
    strict = same & ((c_i > r_i) if reverse else (c_i < r_i))
    ones_where = lambda msk: jnp.where(msk, 1.0, 0.0).astype(BF16)
    g_cum = _dot_exact_lhs(ones_where(incl), g_log)
    g_tot = _dot_exact_lhs(ones_where(same), g_log)
    e_cum = jnp.exp(g_cum)
    e_rest = jnp.exp(g_tot - g_cum)
    e_tot = jnp.exp(g_tot)
    g_cum_t = jnp.transpose(g_cum)
    eye = jnp.where(r_i == c_i, 1.0, 0.0)
    dt = (((1,), (1,)), ((), ()))

    s_cur = [s_scr[h] for h in range(N_HEADS)]
    if finish:
        o_other, z_all = oother_ref[0], z_ref[0]
    heads = range(N_HEADS)
    lane_of = [direction * 8 + h for h in heads]
    col = lambda a, lane: jnp.broadcast_to(a[:, lane:lane + 1], (tb, HD))
    qs, ks, vbs, kbs, attns, ys, ts = [], [], [], [], [], [], []
    for h in heads:
        q = qkv[:, h * HD:(h + 1) * HD]
        k = qkv[:, GROUP_WIDTH + h * HD:GROUP_WIDTH + (h + 1) * HD]
        v = qkv[:, 2 * GROUP_WIDTH + h * HD:2 * GROUP_WIDTH + (h + 1) * HD]
        q = q * lax.rsqrt(jnp.sum(q * q, axis=-1, keepdims=True) + 1e-6) * HD ** -0.5
        k = k * lax.rsqrt(jnp.sum(k * k, axis=-1, keepdims=True) + 1e-6)
        beta = col(beta_all, lane_of[h] + 4)
        kb = k * beta
        k16 = k.astype(BF16)
        decay = jnp.exp(jnp.where(incl, col(g_cum, lane_of[h])[:, :tb] - g_cum_t[lane_of[h]:lane_of[h] + 1, :],
                                  -jnp.inf))
        m_low = jnp.where(strict, lax.dot_general(kb.astype(BF16), k16, dt, preferred_element_type=F32) * decay, 0.0)
        attns.append((lax.dot_general(q.astype(BF16), k16, dt, preferred_element_type=F32) * decay).astype(BF16))
        qs.append(q)
        ks.append(k)
        vbs.append(v * beta)
        kbs.append(kb)
        ys.append(-m_low)
        ts.append(eye - m_low)
    for _ in range(5):
        ys = [_dot3(y, y) for y in ys]
        ts = [t + _dot3(t, y) for t, y in zip(ts, ys)]
    us, ws, k_decs, q_decs = [], [], [], []
    for h in heads:
        rhs = jnp.concatenate([vbs[h], kbs[h] * col(e_cum, lane_of[h])], axis=1).astype(BF16)
        uw = jnp.dot(ts[h].astype(BF16), rhs, preferred_element_type=F32)
        us.append(uw[:, :HD])
        ws.append(uw[:, HD:].astype(BF16))
        k_decs.append((ks[h] * col(e_rest, lane_of[h])).astype(BF16))
        q_decs.append((qs[h] * col(e_cum, lane_of[h])).astype(BF16))
    n_ch = tb // C
    v_new = [[None] * n_ch for _ in heads]
    o_inter = [[None] * n_ch for _ in heads]
    for ci in (range(n_ch - 1, -1, -1) if reverse else range(n_ch)):
        rows = slice(ci * C, (ci + 1) * C)
        for h in heads:
            s16 = s_cur[h].astype(BF16)
            v_new[h][ci] = us[h][rows] - jnp.dot(ws[h][rows], s16, preferred_element_type=F32)
            o_inter[h][ci] = jnp.dot(q_decs[h][rows], s16, preferred_element_type=F32)
        for h in heads:
            s_cur[h] = s_cur[h] * col(e_tot, lane_of[h])[ci * C:ci * C + 1] + lax.dot_general(
                k_decs[h][rows], v_new[h][ci].astype(BF16), (((0,), (0,)), ((), ())), preferred_element_type=F32)
    outs = []
    for h in heads:
        v_all = jnp.concatenate(v_new[h], axis=0).astype(BF16)
        o = jnp.concatenate(o_inter[h], axis=0) + jnp.dot(attns[h], v_all, preferred_element_type=F32)
        if finish:
            o = o + o_other[:, h * HD:(h + 1) * HD]
            o = o * lax.rsqrt(jnp.mean(o * o, axis=-1, keepdims=True) + 1e-6) * ng_ref[...]
            z = z_all[:, h * HD:(h + 1) * HD]
            o = o * (z * jax.nn.sigmoid(z))
        outs.append(o)
    out_ref[0] = jnp.concatenate(outs, axis=1)
    s_fin = jnp.stack(s_cur)
    s_scr[...] = s_fin
    state_ref[0] = s_fin


def gdn_scan(p, pg, s0, cw, ea, dt, ng, direction, o_other=None):
    B, n, _ = p.shape
    G = GROUP_WIDTH
    reverse = direction == 1
    tb = _pick_tile(n, (128,))
    nb = n // tb
    finish = o_other is not None
    tix = (lambda j: nb - 1 - j) if reverse else (lambda j: j)
    r8 = tb // 8
    vec = lambda w: pl.BlockSpec((1, w), lambda b, j: (0, 0))
    in_specs = [pl.BlockSpec((1, N_HEADS, HEAD_DIM, HEAD_DIM), lambda b, j: (b, 0, 0, 0)),
                pl.BlockSpec((1, 8, 3 * G), lambda b, j: (b, jnp.maximum(tix(j) * r8 - 1, 0), 0)),
                pl.BlockSpec((1, tb, 3 * G), lambda b, j: (b, tix(j), 0)),
                pl.BlockSpec((1, 8, 3 * G), lambda b, j: (b, jnp.minimum((tix(j) + 1) * r8, n // 8 - 1), 0)),
                pl.BlockSpec((1, tb, 128), lambda b, j: (b, tix(j), 0)),
                pl.BlockSpec((4, 3 * G), lambda b, j: (0, 0)),
                vec(128), vec(128), vec(HEAD_DIM)]
    args = [s0, p, p, p, pg, cw, ea, dt, ng]
    if finish:
        in_specs += [pl.BlockSpec((1, tb, G), lambda b, j: (b, tix(j), 0)),
                     pl.BlockSpec((1, tb, G), lambda b, j: (b, tix(j), 3))]
        args += [o_other, p]
    kern = functools.partial(_gdn_kernel, tb=tb, n_blocks=nb, reverse=reverse, finish=finish, direction=direction)
    return pl.pallas_call(
        kern,
        grid=(B, nb),
        in_specs=in_specs,
        out_specs=[pl.BlockSpec((1, tb, G), lambda b, j: (b, tix(j), 0)),
                   pl.BlockSpec((1, N_HEADS, HEAD_DIM, HEAD_DIM), lambda b, j: (b, 0, 0, 0))],
        out_shape=[jax.ShapeDtypeStruct((B, n, G), F32),
                   jax.ShapeDtypeStruct((B, N_HEADS, HEAD_DIM, HEAD_DIM), F32)],
        scratch_shapes=[pltpu.VMEM((N_HEADS, HEAD_DIM, HEAD_DIM), F32)],
        compiler_params=_params(2),
        name="gdn_scan",
    )(*args)


def gdn_mixer(p_ctx, g_ctx, p_lat, g_lat, conv_w, a_log, dt_bias, norm_g):
    B = p_lat.shape[0]
    cw = conv_w.T.astype(F32)
    zeros4 = jnp.zeros((2, N_HEADS), F32)
    lanes = lambda t: jnp.pad(jnp.stack([t, zeros4], axis=1).reshape(1, -1), ((0, 0), (0, 128 - 4 * N_HEADS)))
    ea = lanes(jnp.exp(a_log.astype(F32)))
    dt = lanes(dt_bias.astype(F32))
    ng = norm_g.reshape(1, HEAD_DIM).astype(F32)
    s0 = jnp.zeros((B, N_HEADS, HEAD_DIM, HEAD_DIM), F32)
    oc_f, sc_f = gdn_scan(p_ctx, g_ctx, s0, cw, ea, dt, ng, 0)
    ol_f, _ = gdn_scan(p_lat, g_lat, sc_f, cw, ea, dt, ng, 0)
    out_c, sc_b = gdn_scan(p_ctx, g_ctx, s0, cw, ea, dt, ng, 1, o_other=oc_f)
    out_l, _ = gdn_scan(p_lat, g_lat, sc_b, cw, ea, dt, ng, 1, o_other=ol_f)
    return out_c, out_l


def _moe_kernel(be_ref, nused_ref, x_ref, w1_ref, w3_ref, w2_ref, wr_ref, y_ref):
    i = pl.program_id(0)

    @pl.when(i < nused_ref[0])
    def _():
        x = x_ref[...].astype(BF16)
        h1 = jnp.dot(x, w1_ref[0], preferred_element_type=F32)
        h3 = jnp.dot(x, w3_ref[0], preferred_element_type=F32)
        hid = (h1 * jax.nn.sigmoid(h1) * h3).astype(BF16)
        y_ref[...] = jnp.dot(hid, w2_ref[0], preferred_element_type=F32) * wr_ref[...]

    @pl.when(i >= nused_ref[0])
    def _():
        y_ref[...] = jnp.zeros(y_ref.shape, y_ref.dtype)


def grouped_experts(x_sorted, blk_e, n_used, wr, w1, w3, w2, tm):
    R, D = x_sorted.shape
    FF = w1.shape[-1]
    grid_spec = pltpu.PrefetchScalarGridSpec(
        num_scalar_prefetch=2,
        grid=(R // tm,),
        in_specs=[pl.BlockSpec((tm, D), lambda i, be, nu: (i, 0)),
                  pl.BlockSpec((1, D, FF), lambda i, be, nu: (be[i], 0, 0)),
                  pl.BlockSpec((1, D, FF), lambda i, be, nu: (be[i], 0, 0)),
                  pl.BlockSpec((1, FF, D), lambda i, be, nu: (be[i], 0, 0)),
                  pl.BlockSpec((tm, 1), lambda i, be, nu: (i, 0))],
        out_specs=pl.BlockSpec((tm, D), lambda i, be, nu: (i, 0)),
    )
    return pl.pallas_call(
        _moe_kernel,
        grid_spec=grid_spec,
        out_shape=jax.ShapeDtypeStruct((R, D), F32),
        compiler_params=_params(1),
        name="grouped_experts",
    )(blk_e, n_used, x_sorted, w1, w3, w2, wr)


def _moe_finish_kernel(yg_ref, tok_ref, xmid_ref, w1_ref, w3_ref, w2_ref, g2_ref, lg_ref, lb_ref, o_ref, *, alpha):
    x = tok_ref[...].astype(BF16)
    h1 = jnp.dot(x, w1_ref[...], preferred_element_type=F32)
    h3 = jnp.dot(x, w3_ref[...], preferred_element_type=F32)
    hid = (h1 * jax.nn.sigmoid(h1) * h3).astype(BF16)
    y = jnp.sum(yg_ref[...], axis=0) + jnp.dot(hid, w2_ref[...], preferred_element_type=F32)
    o_ref[...] = _layer_norm_rows(alpha * xmid_ref[...] + g2_ref[0] * y, lg_ref[...], lb_ref[...], 1e-5)


def moe_finish(yg, tok, x_mid, sw1, sw3, sw2, g2, ln_g, ln_b, alpha, rows_per_group):
    M, D = tok.shape
    FF = sw1.shape[-1]
    tm = _pick_tile(rows_per_group, (128,))
    tiles_per_group = rows_per_group // tm
    row = pl.BlockSpec((tm, D), lambda i: (i, 0))
    vec = pl.BlockSpec((1, D), lambda i: (0, 0))
    return pl.pallas_call(
        functools.partial(_moe_finish_kernel, alpha=alpha),
        grid=(M // tm,),
        in_specs=[pl.BlockSpec((TOP_K, tm, D), lambda i: (0, i, 0)), row, row,
                  pl.BlockSpec((D, FF), lambda i: (0, 0)),
                  pl.BlockSpec((D, FF), lambda i: (0, 0)),
                  pl.BlockSpec((FF, D), lambda i: (0, 0)),
                  pl.BlockSpec((1, 1, D), lambda i: (i // tiles_per_group, 0, 0)), vec, vec],
        out_specs=row,
        out_shape=jax.ShapeDtypeStruct((M, D), F32),
        compiler_params=_params(1),
        name="moe_finish",
    )(yg, tok, x_mid, sw1, sw3, sw2, g2, ln_g.reshape(1, D), ln_b.reshape(1, D))


def moe_routed(tok, router_w, router_bias, w1, w3, w2):
    T, D = tok.shape
    tm = 512 if T * TOP_K >= 64 * 512 else 128
    scores = jax.nn.sigmoid(jnp.matmul(tok, router_w, preferred_element_type=F32))
    _, idx = lax.top_k(scores + router_bias.astype(F32), TOP_K)
    wts = jnp.take_along_axis(scores, idx, -1)
    wts = wts / jnp.sum(wts, -1, keepdims=True) * ROUTED_SCALE
    A = T * TOP_K
    flat_e = idx.reshape(-1).astype(jnp.int32)
    aid = jnp.arange(A, dtype=jnp.int32)
    _, order, w_sorted = lax.sort((flat_e * A + aid, aid, wts.reshape(-1)), num_keys=1)
    counts = jnp.sum((flat_e[:, None] == jnp.arange(N_EXPERTS, dtype=jnp.int32)[None, :]).astype(jnp.int32), axis=0)
    ends = jnp.cumsum(counts)
    starts = ends - counts
    padded = (counts + tm - 1) // tm * tm
    pad_ends = jnp.cumsum(padded)
    pad_starts = pad_ends - padded
    gap_before = pad_starts - starts
    gap_step = jnp.diff(gap_before, prepend=0)
    shift = jnp.sum(jnp.where(aid[:, None] >= starts[None, :], gap_step[None, :], 0), axis=1)
    _, dest = lax.sort((order, aid + shift), num_keys=1)
    n_blocks = -(-A // tm) + N_EXPERTS
    R = n_blocks * tm
    blk_e = jnp.minimum(jnp.sum((jnp.arange(n_blocks, dtype=jnp.int32)[:, None] * tm >= pad_ends[None, :])
                                .astype(jnp.int32), axis=1), N_EXPERTS - 1)
    n_used = (pad_ends[-1] // tm).astype(jnp.int32).reshape(1)
    row = jnp.arange(R, dtype=jnp.int32).reshape(n_blocks, tm)
    src = row - gap_before[blk_e][:, None]
    valid = (row < (pad_starts + counts)[blk_e][:, None]) & (row < pad_ends[-1])
    src = jnp.where(valid, src, 0).reshape(R)
    buf_tok = jnp.where(valid.reshape(R), order[src] // TOP_K, 0)
    buf_w = jnp.where(valid.reshape(R), w_sorted[src], 0.0)
    y_sorted = grouped_experts(tok[buf_tok], blk_e, n_used, buf_w.reshape(R, 1), w1, w3, w2, tm)
    return y_sorted, dest.reshape(T, TOP_K)


def layer_norm(t, g, b, eps=1e-5):
    mu = jnp.mean(t, -1, keepdims=True)
    var = jnp.mean(jnp.square(t - mu), -1, keepdims=True)
    return (t - mu) * lax.rsqrt(var + eps) * g + b


def rms_norm(t, g, eps=1e-6):
    return t * lax.rsqrt(jnp.mean(jnp.square(t), -1, keepdims=True) + eps) * g


def l2norm(t, eps=1e-6):
    return t * lax.rsqrt(jnp.sum(jnp.square(t), -1, keepdims=True) + eps)


def depthwise_conv(t, w, pad_left, pad_right):
    return lax.conv_general_dilated(t, w.T[:, None, :].astype(t.dtype), window_strides=(1,),
                                    padding=[(pad_left, pad_right)],
                                    dimension_numbers=('NWC', 'WIO', 'NWC'),
                                    feature_group_count=t.shape[-1])


def axial_rope(n_lat, dim):
    rows = n_lat // GRID_W
    n_freq = dim // 4
    inv = ROPE_BASE ** (-jnp.arange(n_freq, dtype=F32) / n_freq)
    row_pos = jnp.repeat(jnp.arange(rows, dtype=F32), GRID_W)
    col_pos = (jnp.arange(n_lat) % GRID_W).astype(F32)
    ang = jnp.concatenate([row_pos[:, None] * inv, col_pos[:, None] * inv], -1)
    return jnp.cos(ang), jnp.sin(ang)


def apply_rope(t, cos, sin):
    tf = t.reshape(*t.shape[:-1], -1, 2)
    t1, t2 = tf[..., 0], tf[..., 1]
    out = jnp.stack([t1 * cos - t2 * sin, t1 * sin + t2 * cos], -1)
    return out.reshape(t.shape)


def diff_attn_mixer(p_ctx, p_lat, rope_cos, rope_sin, lam_vecs, norm_g, layer_idx, with_ctx):
    G = GROUP_WIDTH

    def split_heads(p):
        B, n, _ = p.shape
        q = p[..., :G].reshape(B, n, N_HEADS, 2, DA_QK)
        k = p[..., G:2 * G].reshape(B, n, N_HEADS, 2, DA_QK)
        return q, k, p[..., 2 * G:]

    qc, kc, vc = split_heads(p_ctx)
    ql, kl, vl = split_heads(p_lat)
    B, n = ql.shape[:2]
    L = qc.shape[1]
    cos = rope_cos[None, :, None, None, :]
    sin = rope_sin[None, :, None, None, :]
    scale = DA_QK ** -0.5
    ql = (apply_rope(ql, cos, sin) * scale).reshape(B, n, G).astype(BF16)
    kl = apply_rope(kl, cos, sin).reshape(B, n, G)
    kc = kc.reshape(B, L, G)
    lam_init = 0.8 - 0.6 * math.exp(-0.3 * layer_idx)
    lv = lam_vecs.astype(F32)
    lam = jnp.exp(jnp.sum(lv[0] * lv[1])) - jnp.exp(jnp.sum(lv[2] * lv[3])) + lam_init
    k_all = jnp.concatenate([kc, kl], 1).astype(BF16)
    v_all = jnp.concatenate([vc, vl], 1).astype(BF16)
    out_l = diff_attention(ql, jnp.swapaxes(k_all, 1, 2), v_all, lam, norm_g, 1.0 - lam_init)
    out_c = None
    if with_ctx:
        out_c = diff_attention((qc * scale).reshape(B, L, G).astype(BF16), jnp.swapaxes(kc.astype(BF16), 1, 2),
                               vc.astype(BF16), lam, norm_g, 1.0 - lam_init)
    return out_c, out_l


def short_conv_mixer(p, conv_w):
    b_g, c_g, xs = jnp.split(p, 3, -1)
    return b_g * depthwise_conv(c_g * xs, conv_w, 1, 1)


def kernel(x, c, ctx, c_ctx, w_ada, b_ada, w_in, dn_conv, dn_a_log, dn_dt_bias, dn_norm, lru_conv, lru_conv_b, lru_wa, lru_ba, lru_wx, lru_bx, lru_lambda, da_lambda, da_norm, sc_conv, w_out, ln1_g, ln1_b, router_w, router_bias, exp_w1, exp_w3, exp_w2, sh_w1, sh_w3, sh_w2, ln2_g, ln2_b):
    depth = w_in.shape[0]
    alpha = (2 * depth) ** 0.25
    B, N, D = x.shape
    L = ctx.shape[1]
    G = GROUP_WIDTH
    silu_c = jax.nn.silu(c)[:, None, :]
    silu_cc = jax.nn.silu(c_ctx)[None, None, :]
    rope_cos, rope_sin = axial_rope(N, DA_QK)
    x = x.reshape(B * N, D)
    xc = ctx.reshape(B * L, D)
    n_gate = 4 * N_HEADS
    for l in range(depth):
        with_ctx = l < depth - 1
        sh1, s1, g1, sh2, s2, g2 = jnp.split(silu_c @ w_ada[l] + b_ada[l], 6, -1)
        sh1c, s1c, g1c, sh2c, s2c, g2c = jnp.split(silu_cc @ w_ada[l] + b_ada[l], 6, -1)
        w_l = w_in[l]
        w_main = jnp.concatenate([w_l[:, :4 * G], w_l[:, 4 * G + n_gate:]], axis=1).astype(BF16)
        w_gate = jnp.pad(w_l[:, 4 * G:4 * G + n_gate], ((0, 0), (0, 128 - n_gate))).astype(BF16)
        pl_all = modulated_matmul(x, s1, sh1, w_main, N).reshape(B, N, -1)
        pc_all = modulated_matmul(xc, s1c, sh1c, w_main, B * L).reshape(B, L, -1)
        gl_all = modulated_matmul(x, s1, sh1, w_gate, N).reshape(B, N, -1)
        gc_all = modulated_matmul(xc, s1c, sh1c, w_gate, B * L).reshape(B, L, -1)
        a_c, a_l = gdn_mixer(pc_all, gc_all, pl_all, gl_all, dn_conv[l], dn_a_log[l], dn_dt_bias[l], dn_norm[l])
        b_c, b_l = rglru_mixer(pc_all, pl_all, 4, 5, lru_conv[l], lru_conv_b[l], lru_wa[l], lru_ba[l],
                               lru_wx[l], lru_bx[l], lru_lambda[l])
        c_c, c_l = diff_attn_mixer(pc_all[..., 6 * G:9 * G], pl_all[..., 6 * G:9 * G], rope_cos, rope_sin,
                                   da_lambda[l], da_norm[l], l, with_ctx)
        d_l = short_conv_mixer(pl_all[..., 9 * G:12 * G], sc_conv[l])
        w_out_l = w_out[l].astype(BF16)
        flat = lambda t: t.reshape(-1, t.shape[-1])
        x_mid, tok_l = out_proj_norm([flat(a_l), flat(b_l), flat(c_l), flat(d_l)], w_out_l, x, g1, s2, sh2,
                                     ln1_g[l], ln1_b[l], alpha, N)
        if with_ctx:
            d_c = short_conv_mixer(pc_all[..., 9 * G:12 * G], sc_conv[l])
            xc_mid, tok_c = out_proj_norm([flat(a_c), flat(b_c), flat(c_c), flat(d_c)], w_out_l, xc, g1c, s2c, sh2c,
                                          ln1_g[l], ln1_b[l], alpha, B * L)
            tok = jnp.concatenate([tok_c, tok_l], 0)
        else:
            tok = tok_l
        y_sorted, dest = moe_routed(tok, router_w[l], router_bias[l], exp_w1[l].astype(BF16),
                                    exp_w3[l].astype(BF16), exp_w2[l].astype(BF16))
        shared_w = (sh_w1[l].astype(BF16), sh_w3[l].astype(BF16), sh_w2[l].astype(BF16))
        if with_ctx:
            xc = moe_finish(y_sorted[dest[:B * L].T], tok_c, xc_mid, *shared_w, g2c, ln2_g[l], ln2_b[l], alpha, B * L)
            dest = dest[B * L:]
        x = moe_finish(y_sorted[dest.T], tok_l, x_mid, *shared_w, g2, ln2_g[l], ln2_b[l], alpha, N)
    return x.reshape(B, N, D)
```

```python
import functools
import math

import jax
import jax.numpy as jnp
from jax import lax
from jax.experimental import pallas as pl
from jax.experimental.pallas import tpu as pltpu

F32 = jnp.float32
BF16 = jnp.bfloat16

GRID_W = 64
GROUP_WIDTH = 512
HEAD_DIM = 128
N_HEADS = GROUP_WIDTH // HEAD_DIM
DN_CHUNK = 64
LRU_C = 8.0
DA_QK = HEAD_DIM // 2
ROPE_BASE = 10000.0
N_EXPERTS = 64
TOP_K = 6
ROUTED_SCALE = 2.5

ATTN_TQ = (1024, 512, 256, 128)
ATTN_TK = (1280, 512, 256, 128)
ATTN_UNROLL = 2

V7X_VMEM_BYTES = 64 * 1024 * 1024
VMEM_LIMIT = V7X_VMEM_BYTES * 3 // 4


def _params(n_axes):
    return pltpu.CompilerParams(dimension_semantics=("arbitrary",) * n_axes, vmem_limit_bytes=VMEM_LIMIT)


def _pick_tile(n, candidates):
    for c in candidates:
        if n % c == 0:
            return c
    return n


def _mod_mm_kernel(a_ref, sc_ref, sh_ref, w_ref, o_ref, a_scr):
    @pl.when(pl.program_id(1) == 0)
    def _():
        a_scr[...] = (a_ref[...] * (1.0 + sc_ref[0]) + sh_ref[0]).astype(BF16)

    o_ref[...] = jnp.dot(a_scr[...], w_ref[...], preferred_element_type=F32)


def modulated_matmul(a, scale, shift, w, rows_per_group):
    M, K = a.shape
    N = w.shape[1]
    tm = _pick_tile(rows_per_group, (1024, 512, 256, 128))
    tn = _pick_tile(N, (512, 256, 128))
    tiles_per_group = rows_per_group // tm
    grp = lambda i, j: (i // tiles_per_group, 0, 0)
    return pl.pallas_call(
        _mod_mm_kernel,
        grid=(M // tm, N // tn),
        in_specs=[pl.BlockSpec((tm, K), lambda i, j: (i, 0)),
                  pl.BlockSpec((1, 1, K), grp),
                  pl.BlockSpec((1, 1, K), grp),
                  pl.BlockSpec((K, tn), lambda i, j: (0, j))],
        out_specs=pl.BlockSpec((tm, tn), lambda i, j: (i, j)),
        out_shape=jax.ShapeDtypeStruct((M, N), F32),
        scratch_shapes=[pltpu.VMEM((tm, K), BF16)],
        compiler_params=_params(2),
        name="modulated_matmul",
    )(a, scale, shift, w)


def _layer_norm_rows(r, g, b, eps):
    mu = jnp.mean(r, axis=-1, keepdims=True)
    d = r - mu
    return d * lax.rsqrt(jnp.mean(d * d, axis=-1, keepdims=True) + eps) * g + b


def _out_proj_kernel(a_ref, b_ref, c_ref, d_ref, w_ref, x_ref, g1_ref, s2_ref, sh2_ref, lg_ref, lb_ref,
                     xmid_ref, tok_ref, *, alpha):
    G = GROUP_WIDTH
    mix = sum(jnp.dot(r[...].astype(BF16), w_ref[g * G:(g + 1) * G, :], preferred_element_type=F32)
              for g, r in enumerate((a_ref, b_ref, c_ref, d_ref)))
    x_mid = _layer_norm_rows(alpha * x_ref[...] + g1_ref[0] * mix, lg_ref[...], lb_ref[...], 1e-5)
    xmid_ref[...] = x_mid
    tok_ref[...] = x_mid * (1.0 + s2_ref[0]) + sh2_ref[0]


def out_proj_norm(mix_parts, w, x, g1, s2, sh2, ln_g, ln_b, alpha, rows_per_group):
    M, D = x.shape
    G = GROUP_WIDTH
    tm = _pick_tile(rows_per_group, (256, 128))
    tiles_per_group = rows_per_group // tm
    grp = lambda i: (i // tiles_per_group, 0, 0)
    row = lambda width: pl.BlockSpec((tm, width), lambda i: (i, 0))
    vec = pl.BlockSpec((1, D), lambda i: (0, 0))
    return pl.pallas_call(
        functools.partial(_out_proj_kernel, alpha=alpha),
        grid=(M // tm,),
        in_specs=[row(G), row(G), row(G), row(G), pl.BlockSpec((D, D), lambda i: (0, 0)), row(D),
                  pl.BlockSpec((1, 1, D), grp), pl.BlockSpec((1, 1, D), grp), pl.BlockSpec((1, 1, D), grp), vec, vec],
        out_specs=[row(D), row(D)],
        out_shape=[jax.ShapeDtypeStruct((M, D), F32), jax.ShapeDtypeStruct((M, D), F32)],
        compiler_params=_params(1),
        name="out_proj_norm",
    )(*mix_parts, w, x, g1, s2, sh2, ln_g.reshape(1, D), ln_b.reshape(1, D))


def _attn_kernel(lam_ref, g_ref, q_ref, kt_ref, v_ref, o_ref, m_scr, acc_scr, *, tk, n_chunks, out_scale):
    q = q_ref[0]
    lane = lax.broadcasted_iota(jnp.int32, q.shape, 1)
    zero = jnp.zeros_like(q)
    q_maps = (jnp.where(lane < DA_QK, q, zero), jnp.where(lane >= DA_QK, q, zero))
    m_scr[...] = jnp.full(m_scr.shape, -jnp.inf, F32)
    acc_scr[...] = jnp.zeros(acc_scr.shape, F32)
    ones = jnp.ones((tk, HEAD_DIM), BF16)

    def chunk(c, carry):
        off = pl.multiple_of(c * tk, tk)
        ktc = kt_ref[0, :, pl.ds(off, tk)]
        v_ext = jnp.concatenate([v_ref[0, pl.ds(off, tk), :], ones], axis=1)
        for mp in range(2):
            s = jnp.dot(q_maps[mp], ktc, preferred_element_type=F32)
            m_prev = m_scr[mp]
            m_new = jnp.maximum(m_prev, jnp.max(s, axis=-1, keepdims=True))
            p = jnp.exp(s - pltpu.repeat(m_new, tk // HEAD_DIM, axis=1))
            alpha = jnp.exp(m_prev - m_new)
            acc_scr[mp] = (pltpu.repeat(alpha, 2, axis=1) * acc_scr[mp]
                           + jnp.dot(p.astype(BF16), v_ext, preferred_element_type=F32))
            m_scr[mp] = m_new
        return carry

    lax.fori_loop(0, n_chunks, chunk, 0, unroll=ATTN_UNROLL)
    a0, a1 = acc_scr[0], acc_scr[1]
    o = a0[:, :HEAD_DIM] / a0[:, HEAD_DIM:] - lam_ref[...] * (a1[:, :HEAD_DIM] / a1[:, HEAD_DIM:])
    o = o * lax.rsqrt(jnp.mean(o * o, axis=-1, keepdims=True) + 1e-6)
    o_ref[0] = o * g_ref[...] * out_scale


def diff_attention(q, kt, v, lam, norm_g, out_scale):
    B, n, G = q.shape
    m = v.shape[1]
    tq = _pick_tile(n, ATTN_TQ)
    tk = _pick_tile(m, ATTN_TK)
    kern = functools.partial(_attn_kernel, tk=tk, n_chunks=m // tk, out_scale=out_scale)
    lam_row = jnp.full((1, HEAD_DIM), lam, F32)
    return pl.pallas_call(
        kern,
        grid=(B, N_HEADS, n // tq),
        in_specs=[pl.BlockSpec((1, HEAD_DIM), lambda b, h, i: (0, 0)),
                  pl.BlockSpec((1, HEAD_DIM), lambda b, h, i: (0, 0)),
                  pl.BlockSpec((1, tq, HEAD_DIM), lambda b, h, i: (b, i, h)),
                  pl.BlockSpec((1, HEAD_DIM, m), lambda b, h, i: (b, h, 0)),
                  pl.BlockSpec((1, m, HEAD_DIM), lambda b, h, i: (b, 0, h))],
        out_specs=pl.BlockSpec((1, tq, HEAD_DIM), lambda b, h, i: (b, i, h)),
        out_shape=jax.ShapeDtypeStruct((B, n, G), F32),
        scratch_shapes=[pltpu.VMEM((2, tq, HEAD_DIM), F32), pltpu.VMEM((2, tq, 2 * HEAD_DIM), F32)],
        compiler_params=_params(3),
        name="diff_attention",
    )(lam_row, norm_g.reshape(1, HEAD_DIM).astype(F32), q, kt, v)


def _lru_kernel(h0_ref, xprev_ref, x_ref, xnext_ref, cw_ref, cb_ref, wg_ref, bg_ref, sp_ref, *rest,
                tb, n_blocks, reverse, finish):
    if finish:
        hother_ref, gate_ref, out_ref, state_ref, a_scr, b_scr, h_scr = rest
    else:
        out_ref, state_ref, a_scr, b_scr, h_scr = rest
    j = pl.program_id(1)
    t_blk = (n_blocks - 1 - j) if reverse else j
    W = x_ref.shape[-1]

    @pl.when(j == 0)
    def _():
        h_scr[...] = jnp.broadcast_to(h0_ref[0], h_scr.shape)

    prev = jnp.where(t_blk > 0, xprev_ref[0], 0.0)
    nxt = jnp.where(t_blk < n_blocks - 1, xnext_ref[0], 0.0)
    x_ext = jnp.concatenate([prev, x_ref[0], nxt], axis=0)
    cw = cw_ref[...]
    xb = cb_ref[...] + sum(cw[k:k + 1] * x_ext[6 + k:6 + k + tb] for k in range(4))
    xb16 = xb.astype(BF16)
    hw = W // N_HEADS
    pre = [jnp.dot(xb16[:, h * hw:(h + 1) * hw], wg_ref[h], preferred_element_type=F32) for h in range(N_HEADS)]
    pre_r = jnp.concatenate([p[:, :hw] for p in pre], axis=1) + bg_ref[0:1]
    pre_i = jnp.concatenate([p[:, hw:] for p in pre], axis=1) + bg_ref[1:2]
    log_a = -LRU_C * jax.nn.sigmoid(pre_r) * sp_ref[...]
    a = jnp.exp(log_a)
    a_scr[...] = a
    b_scr[...] = jnp.sqrt(-jnp.tanh(log_a) * (a * a + 1.0)) * (jax.nn.sigmoid(pre_i) * xb)

    row = lax.broadcasted_iota(jnp.int32, (8, W), 0)
    n_rows = tb // 8

    def block(i, h):
        blk = (n_rows - 1 - i) if reverse else i
        r0 = pl.multiple_of(blk * 8, 8)
        a = a_scr[pl.ds(r0, 8), :]
        b = b_scr[pl.ds(r0, 8), :]
        for d in (1, 2, 4):
            if reverse:
                keep = row < 8 - d
                a_sh, b_sh = pltpu.roll(a, 8 - d, 0), pltpu.roll(b, 8 - d, 0)
            else:
                keep = row >= d
                a_sh, b_sh = pltpu.roll(a, d, 0), pltpu.roll(b, d, 0)
            b = jnp.where(keep, b + a * b_sh, b)
            a = jnp.where(keep, a * a_sh, a)
        hb = b + a * h
        b_scr[pl.ds(r0, 8), :] = hb
        last = hb[0:1] if reverse else hb[7:8]
        return jnp.broadcast_to(last, (8, W))

    h_fin = lax.fori_loop(0, n_rows, block, h_scr[...], unroll=4)
    h_scr[...] = h_fin
    state_ref[0] = h_fin
    if finish:
        out_ref[0] = (hother_ref[0] + b_scr[...]) * jax.nn.gelu(gate_ref[0])
    else:
        out_ref[0] = b_scr[...]


def lru_scan(p, x_col, gate_col, h0, cw, cb, wg, bg, sp, reverse, h_other=None):
    B, n, _ = p.shape
    W = GROUP_WIDTH
    tb = _pick_tile(n, (512, 256, 128))
    nb = n // tb
    finish = h_other is not None
    tix = (lambda j: nb - 1 - j) if reverse else (lambda j: j)
    r8 = tb // 8
    in_specs = [pl.BlockSpec((1, 8, W), lambda b, j: (b, 0, 0)),
                pl.BlockSpec((1, 8, W), lambda b, j: (b, jnp.maximum(tix(j) * r8 - 1, 0), x_col)),
                pl.BlockSpec((1, tb, W), lambda b, j: (b, tix(j), x_col)),
                pl.BlockSpec((1, 8, W), lambda b, j: (b, jnp.minimum((tix(j) + 1) * r8, n // 8 - 1), x_col)),
                pl.BlockSpec((4, W), lambda b, j: (0, 0)),
                pl.BlockSpec((1, W), lambda b, j: (0, 0)),
                pl.BlockSpec((N_HEADS, W // N_HEADS, 2 * W // N_HEADS), lambda b, j: (0, 0, 0)),
                pl.BlockSpec((2, W), lambda b, j: (0, 0)),
                pl.BlockSpec((1, W), lambda b, j: (0, 0))]
    args = [h0, p, p, p, cw, cb, wg, bg, sp]
    if finish:
        in_specs += [pl.BlockSpec((1, tb, W), lambda b, j: (b, tix(j), 0)),
                     pl.BlockSpec((1, tb, W), lambda b, j: (b, tix(j), gate_col))]
        args += [h_other, p]
    kern = functools.partial(_lru_kernel, tb=tb, n_blocks=nb, reverse=reverse, finish=finish)
    return pl.pallas_call(
        kern,
        grid=(B, nb),
        in_specs=in_specs,
        out_specs=[pl.BlockSpec((1, tb, W), lambda b, j: (b, tix(j), 0)),
                   pl.BlockSpec((1, 8, W), lambda b, j: (b, 0, 0))],
        out_shape=[jax.ShapeDtypeStruct((B, n, W), F32), jax.ShapeDtypeStruct((B, 8, W), F32)],
        scratch_shapes=[pltpu.VMEM((tb, W), F32), pltpu.VMEM((tb, W), F32), pltpu.VMEM((8, W), F32)],
        compiler_params=_params(2),
        name="lru_scan",
    )(*args)


def rglru_mixer(p_ctx, p_lat, x_col, gate_col, conv_w, conv_b, wa, ba, wx, bx, lam):
    B = p_lat.shape[0]
    W = GROUP_WIDTH
    cw = conv_w.T.astype(F32)
    cb = conv_b.reshape(1, W).astype(F32)
    h0 = jnp.zeros((B, 8, W), F32)
    outs = []
    states = {}
    for d, reverse in ((0, False), (1, True)):
        wg = jnp.concatenate([wa[d], wx[d]], axis=-1).astype(BF16)
        bg = jnp.stack([ba[d], bx[d]]).astype(F32)
        sp = jax.nn.softplus(-lam[d]).reshape(1, W).astype(F32)
        states[d] = (wg, bg, sp)
    wg, bg, sp = states[0]
    hc_f, sc_f = lru_scan(p_ctx, x_col, gate_col, h0, cw, cb, wg, bg, sp, False)
    hl_f, _ = lru_scan(p_lat, x_col, gate_col, sc_f, cw, cb, wg, bg, sp, False)
    wg, bg, sp = states[1]
    out_c, sc_b = lru_scan(p_ctx, x_col, gate_col, h0, cw, cb, wg, bg, sp, True, h_other=hc_f)
    out_l, _ = lru_scan(p_lat, x_col, gate_col, sc_b, cw, cb, wg, bg, sp, True, h_other=hl_f)
    return out_c, out_l


def _split_bf16(a):
    hi = a.astype(BF16)
    return hi, (a - hi.astype(F32)).astype(BF16)


def _dot3(a, b):
    ah, al = _split_bf16(a)
    bh, bl = _split_bf16(b)
    d = functools.partial(jnp.dot, preferred_element_type=F32)
    return d(ah, bh) + (d(ah, bl) + d(al, bh))


def _dot_exact_lhs(a_exact, b):
    bh, bl = _split_bf16(b)
    d = functools.partial(jnp.dot, preferred_element_type=F32)
    return d(a_exact, bh) + d(a_exact, bl)


def _gdn_kernel(s0_ref, xprev_ref, x_ref, xnext_ref, gate_ref, cw_ref, ea_ref, dt_ref, ng_ref, *rest,
                tb, n_blocks, reverse, finish, direction):
    if finish:
        oother_ref, z_ref, out_ref, state_ref, s_scr = rest
    else:
        out_ref, state_ref, s_scr = rest
    C = DN_CHUNK
    HD = HEAD_DIM
    j = pl.program_id(1)
    t_blk = (n_blocks - 1 - j) if reverse else j

    @pl.when(j == 0)
    def _():
        s_scr[...] = s0_ref[0]

    prev = jnp.where(t_blk > 0, xprev_ref[0], 0.0)
    nxt = jnp.where(t_blk < n_blocks - 1, xnext_ref[0], 0.0)
    x_ext = jnp.concatenate([prev, x_ref[0], nxt], axis=0)
    cw = cw_ref[...]
    qkv = sum(cw[k:k + 1] * x_ext[6 + k:6 + k + tb] for k in range(4))
    qkv = qkv * jax.nn.sigmoid(qkv)

    gl_in = gate_ref[0]
    g_log = -ea_ref[...] * jax.nn.softplus(gl_in + dt_ref[...])
    beta_all = jax.nn.sigmoid(gl_in)
    r_i = lax.broadcasted_iota(jnp.int32, (tb, tb), 0)
    c_i = lax.broadcasted_iota(jnp.int32, (tb, tb), 1)
    same = (r_i // C) == (c_i // C)
    incl = same & ((c_i >= r_i) if reverse else (c_i <= r_i))
    strict = same & ((c_i > r_i) if reverse else (c_i < r_i))
    ones_where = lambda msk: jnp.where(msk, 1.0, 0.0).astype(BF16)
    g_cum = _dot_exact_lhs(ones_where(incl), g_log)
    g_tot = _dot_exact_lhs(ones_where(same), g_log)
    e_cum = jnp.exp(g_cum)
    e_rest = jnp.exp(g_tot - g_cum)
    e_tot = jnp.exp(g_tot)
    g_cum_t = jnp.transpose(g_cum)
    eye = jnp.where(r_i == c_i, 1.0, 0.0)
    dt = (((1,), (1,)), ((), ()))

    s_cur = [s_scr[h] for h in range(N_HEADS)]
    if finish:
        o_other, z_all = oother_ref[0], z_ref[0]
    heads = range(N_HEADS)
    lane_of = [direction * 8 + h for h in heads]
    col = lambda a, lane: jnp.broadcast_to(a[:, lane:lane + 1], (tb, HD))
    qs, ks, vbs, kbs, attns, ys, ts = [], [], [], [], [], [], []
    for h in heads:
        q = qkv[:, h * HD:(h + 1) * HD]
        k = qkv[:, GROUP_WIDTH + h * HD:GROUP_WIDTH + (h + 1) * HD]
        v = qkv[:, 2 * GROUP_WIDTH + h * HD:2 * GROUP_WIDTH + (h + 1) * HD]
        q = q * lax.rsqrt(jnp.sum(q * q, axis=-1, keepdims=True) + 1e-6) * HD ** -0.5
        k = k * lax.rsqrt(jnp.sum(k * k, axis=-1, keepdims=True) + 1e-6)
        beta = col(beta_all, lane_of[h] + 4)
        kb = k * beta
        k16 = k.astype(BF16)
        decay = jnp.exp(jnp.where(incl, col(g_cum, lane_of[h])[:, :tb] - g_cum_t[lane_of[h]:lane_of[h] + 1, :],
                                  -jnp.inf))
        m_low = jnp.where(strict, lax.dot_general(kb.astype(BF16), k16, dt, preferred_element_type=F32) * decay, 0.0)
        attns.append((lax.dot_general(q.astype(BF16), k16, dt, preferred_element_type=F32) * decay).astype(BF16))
        qs.append(q)
        ks.append(k)
        vbs.append(v * beta)
        kbs.append(kb)
        ys.append(-m_low)
        ts.append(eye - m_low)
    for _ in range(5):
        ys = [_dot3(y, y) for y in ys]
        ts = [t + _dot3(t, y) for t, y in zip(ts, ys)]
    us, ws, k_decs, q_decs = [], [], [], []
    for h in heads:
        rhs = jnp.concatenate([vbs[h], kbs[h] * col(e_cum, lane_of[h])], axis=1).astype(BF16)
        uw = jnp.dot(ts[h].astype(BF16), rhs, preferred_element_type=F32)
        us.append(uw[:, :HD])
        ws.append(uw[:, HD:].astype(BF16))
        k_decs.append((ks[h] * col(e_rest, lane_of[h])).astype(BF16))
        q_decs.append((qs[h] * col(e_cum, lane_of[h])).astype(BF16))
    n_ch = tb // C
    v_new = [[None] * n_ch for _ in heads]
    o_inter = [[None] * n_ch for _ in heads]
    for ci in (range(n_ch - 1, -1, -1) if reverse else range(n_ch)):
        rows = slice(ci * C, (ci + 1) * C)
        for h in heads:
            s16 = s_cur[h].astype(BF16)
            v_new[h][ci] = us[h][rows] - jnp.dot(ws[h][rows], s16, preferred_element_type=F32)
            o_inter[h][ci] = jnp.dot(q_decs[h][rows], s16, preferred_element_type=F32)
        for h in heads:
            s_cur[h] = s_cur[h] * col(e_tot, lane_of[h])[ci * C:ci * C + 1] + lax.dot_general(
                k_decs[h][rows], v_new[h][ci].astype(BF16), (((0,), (0,)), ((), ())), preferred_element_type=F32)
    outs = []
    for h in heads:
        v_all = jnp.concatenate(v_new[h], axis=0).astype(BF16)
        o = jnp.concatenate(o_inter[h], axis=0) + jnp.dot(attns[h], v_all, preferred_element_type=F32)
        if finish:
            o = o + o_other[:, h * HD:(h + 1) * HD]
            o = o * lax.rsqrt(jnp.mean(o * o, axis=-1, keepdims=True) + 1e-6) * ng_ref[...]
            z = z_all[:, h * HD:(h + 1) * HD]
            o = o * (z * jax.nn.sigmoid(z))
        outs.append(o)
    out_ref[0] = jnp.concatenate(outs, axis=1)
    s_fin = jnp.stack(s_cur)
    s_scr[...] = s_fin
    state_ref[0] = s_fin


def gdn_scan(p, pg, s0, cw, ea, dt, ng, direction, o_other=None):
    B, n, _ = p.shape
    G = GROUP_WIDTH
    reverse = direction == 1
    tb = _pick_tile(n, (128,))
    nb = n // tb
    finish = o_other is not None
    tix = (lambda j: nb - 1 - j) if reverse else (lambda j: j)
    r8 = tb // 8
    vec = lambda w: pl.BlockSpec((1, w), lambda b, j: (0, 0))
    in_specs = [pl.BlockSpec((1, N_HEADS, HEAD_DIM, HEAD_DIM), lambda b, j: (b, 0, 0, 0)),
                pl.BlockSpec((1, 8, 3 * G), lambda b, j: (b, jnp.maximum(tix(j) * r8 - 1, 0), 0)),
                pl.BlockSpec((1, tb, 3 * G), lambda b, j: (b, tix(j), 0)),
                pl.BlockSpec((1, 8, 3 * G), lambda b, j: (b, jnp.minimum((tix(j) + 1) * r8, n // 8 - 1), 0)),
                pl.BlockSpec((1, tb, 128), lambda b, j: (b, tix(j), 0)),
                pl.BlockSpec((4, 3 * G), lambda b, j: (0, 0)),
                vec(128), vec(128), vec(HEAD_DIM)]
    args = [s0, p, p, p, pg, cw, ea, dt, ng]
    if finish:
        in_specs += [pl.BlockSpec((1, tb, G), lambda b, j: (b, tix(j), 0)),
                     pl.BlockSpec((1, tb, G), lambda b, j: (b, tix(j), 3))]
        args += [o_other, p]
    kern = functools.partial(_gdn_kernel, tb=tb, n_blocks=nb, reverse=reverse, finish=finish, direction=direction)
    return pl.pallas_call(
        kern,
        grid=(B, nb),
        in_specs=in_specs,
        out_specs=[pl.BlockSpec((1, tb, G), lambda b, j: (b, tix(j), 0)),
                   pl.BlockSpec((1, N_HEADS, HEAD_DIM, HEAD_DIM), lambda b, j: (b, 0, 0, 0))],
        out_shape=[jax.ShapeDtypeStruct((B, n, G), F32),
                   jax.ShapeDtypeStruct((B, N_HEADS, HEAD_DIM, HEAD_DIM), F32)],
        scratch_shapes=[pltpu.VMEM((N_HEADS, HEAD_DIM, HEAD_DIM), F32)],
        compiler_params=_params(2),
        name="gdn_scan",
    )(*args)


def gdn_mixer(p_ctx, g_ctx, p_lat, g_lat, conv_w, a_log, dt_bias, norm_g):
    B = p_lat.shape[0]
    cw = conv_w.T.astype(F32)
    zeros4 = jnp.zeros((2, N_HEADS), F32)
    lanes = lambda t: jnp.pad(jnp.stack([t, zeros4], axis=1).reshape(1, -1), ((0, 0), (0, 128 - 4 * N_HEADS)))
    ea = lanes(jnp.exp(a_log.astype(F32)))
    dt = lanes(dt_bias.astype(F32))
    ng = norm_g.reshape(1, HEAD_DIM).astype(F32)
    s0 = jnp.zeros((B, N_HEADS, HEAD_DIM, HEAD_DIM), F32)
    oc_f, sc_f = gdn_scan(p_ctx, g_ctx, s0, cw, ea, dt, ng, 0)
    ol_f, _ = gdn_scan(p_lat, g_lat, sc_f, cw, ea, dt, ng, 0)
    out_c, sc_b = gdn_scan(p_ctx, g_ctx, s0, cw, ea, dt, ng, 1, o_other=oc_f)
    out_l, _ = gdn_scan(p_lat, g_lat, sc_b, cw, ea, dt, ng, 1, o_other=ol_f)
    return out_c, out_l


def _moe_kernel(be_ref, nused_ref, x_ref, w1_ref, w3_ref, w2_ref, wr_ref, y_ref, w1_scr, w3_scr, w2_scr):
    i = pl.program_id(0)

    @pl.when((i == 0) | (be_ref[i] != be_ref[jnp.maximum(i - 1, 0)]))
    def _():
        w1_scr[...] = w1_ref[0].astype(BF16)
        w3_scr[...] = w3_ref[0].astype(BF16)
        w2_scr[...] = w2_ref[0].astype(BF16)

    @pl.when(i < nused_ref[0])
    def _():
        x = x_ref[...].astype(BF16)
        h1 = jnp.dot(x, w1_scr[...], preferred_element_type=F32)
        h3 = jnp.dot(x, w3_scr[...], preferred_element_type=F32)
        hid = (h1 * jax.nn.sigmoid(h1) * h3).astype(BF16)
        y_ref[...] = jnp.dot(hid, w2_scr[...], preferred_element_type=F32) * wr_ref[...]

    @pl.when(i >= nused_ref[0])
    def _():
        y_ref[...] = jnp.zeros(y_ref.shape, y_ref.dtype)


def grouped_experts(x_sorted, blk_e, n_used, wr, w1, w3, w2, tm):
    R, D = x_sorted.shape
    FF = w1.shape[-1]
    grid_spec = pltpu.PrefetchScalarGridSpec(
        num_scalar_prefetch=2,
        grid=(R // tm,),
        in_specs=[pl.BlockSpec((tm, D), lambda i, be, nu: (i, 0)),
                  pl.BlockSpec((1, D, FF), lambda i, be, nu: (be[i], 0, 0)),
                  pl.BlockSpec((1, D, FF), lambda i, be, nu: (be[i], 0, 0)),
                  pl.BlockSpec((1, FF, D), lambda i, be, nu: (be[i], 0, 0)),
                  pl.BlockSpec((tm, 1), lambda i, be, nu: (i, 0))],
        out_specs=pl.BlockSpec((tm, D), lambda i, be, nu: (i, 0)),
        scratch_shapes=[pltpu.VMEM((D, FF), BF16), pltpu.VMEM((D, FF), BF16), pltpu.VMEM((FF, D), BF16)],
    )
    return pl.pallas_call(
        _moe_kernel,
        grid_spec=grid_spec,
        out_shape=jax.ShapeDtypeStruct((R, D), F32),
        compiler_params=_params(1),
        name="grouped_experts",
    )(blk_e, n_used, x_sorted, w1, w3, w2, wr)


def _moe_finish_kernel(yg_ref, tok_ref, xmid_ref, w1_ref, w3_ref, w2_ref, g2_ref, lg_ref, lb_ref, o_ref, *, alpha):
    x = tok_ref[...].astype(BF16)
    h1 = jnp.dot(x, w1_ref[...], preferred_element_type=F32)
    h3 = jnp.dot(x, w3_ref[...], preferred_element_type=F32)
    hid = (h1 * jax.nn.sigmoid(h1) * h3).astype(BF16)
    y = jnp.sum(yg_ref[...], axis=0) + jnp.dot(hid, w2_ref[...], preferred_element_type=F32)
    o_ref[...] = _layer_norm_rows(alpha * xmid_ref[...] + g2_ref[0] * y, lg_ref[...], lb_ref[...], 1e-5)


def moe_finish(yg, tok, x_mid, sw1, sw3, sw2, g2, ln_g, ln_b, alpha, rows_per_group):
    M, D = tok.shape
    FF = sw1.shape[-1]
    tm = _pick_tile(rows_per_group, (128,))
    tiles_per_group = rows_per_group // tm
    row = pl.BlockSpec((tm, D), lambda i: (i, 0))
    vec = pl.BlockSpec((1, D), lambda i: (0, 0))
    return pl.pallas_call(
        functools.partial(_moe_finish_kernel, alpha=alpha),
        grid=(M // tm,),
        in_specs=[pl.BlockSpec((TOP_K, tm, D), lambda i: (0, i, 0)), row, row,
                  pl.BlockSpec((D, FF), lambda i: (0, 0)),
                  pl.BlockSpec((D, FF), lambda i: (0, 0)),
                  pl.BlockSpec((FF, D), lambda i: (0, 0)),
                  pl.BlockSpec((1, 1, D), lambda i: (i // tiles_per_group, 0, 0)), vec, vec],
        out_specs=row,
        out_shape=jax.ShapeDtypeStruct((M, D), F32),
        compiler_params=_params(1),
        name="moe_finish",
    )(yg, tok, x_mid, sw1, sw3, sw2, g2, ln_g.reshape(1, D), ln_b.reshape(1, D))


def moe_routed(tok, router_w, router_bias, w1, w3, w2):
    T, D = tok.shape
    tm = 512 if T * TOP_K >= 64 * 512 else 128
    scores = jax.nn.sigmoid(jnp.matmul(tok, router_w, preferred_element_type=F32))
    _, idx = lax.top_k(scores + router_bias.astype(F32), TOP_K)
    wts = jnp.take_along_axis(scores, idx, -1)
    wts = wts / jnp.sum(wts, -1, keepdims=True) * ROUTED_SCALE
    A = T * TOP_K
    flat_e = idx.reshape(-1).astype(jnp.int32)
    aid = jnp.arange(A, dtype=jnp.int32)
    _, order, w_sorted = lax.sort((flat_e * A + aid, aid, wts.reshape(-1)), num_keys=1)
    counts = jnp.sum((flat_e[:, None] == jnp.arange(N_EXPERTS, dtype=jnp.int32)[None, :]).astype(jnp.int32), axis=0)
    ends = jnp.cumsum(counts)
    starts = ends - counts
    padded = (counts + tm - 1) // tm * tm
    pad_ends = jnp.cumsum(padded)
    pad_starts = pad_ends - padded
    gap_before = pad_starts - starts
    gap_step = jnp.diff(gap_before, prepend=0)
    shift = jnp.sum(jnp.where(aid[:, None] >= starts[None, :], gap_step[None, :], 0), axis=1)
    _, dest = lax.sort((order, aid + shift), num_keys=1)
    n_blocks = -(-A // tm) + N_EXPERTS
    R = n_blocks * tm
    blk_e = jnp.minimum(jnp.sum((jnp.arange(n_blocks, dtype=jnp.int32)[:, None] * tm >= pad_ends[None, :])
                                .astype(jnp.int32), axis=1), N_EXPERTS - 1)
    n_used = (pad_ends[-1] // tm).astype(jnp.int32).reshape(1)
    row = jnp.arange(R, dtype=jnp.int32).reshape(n_blocks, tm)
    src = row - gap_before[blk_e][:, None]
    valid = (row < (pad_starts + counts)[blk_e][:, None]) & (row < pad_ends[-1])
    src = jnp.where(valid, src, 0).reshape(R)
    buf_tok = jnp.where(valid.reshape(R), order[src] // TOP_K, 0)
    buf_w = jnp.where(valid.reshape(R), w_sorted[src], 0.0)
    y_sorted = grouped_experts(tok[buf_tok], blk_e, n_used, buf_w.reshape(R, 1), w1, w3, w2, tm)
    return y_sorted, dest.reshape(T, TOP_K)


def layer_norm(t, g, b, eps=1e-5):
    mu = jnp.mean(t, -1, keepdims=True)
    var = jnp.mean(jnp.square(t - mu), -1, keepdims=True)
    return (t - mu) * lax.rsqrt(var + eps) * g + b


def rms_norm(t, g, eps=1e-6):
    return t * lax.rsqrt(jnp.mean(jnp.square(t), -1, keepdims=True) + eps) * g


def l2norm(t, eps=1e-6):
    return t * lax.rsqrt(jnp.sum(jnp.square(t), -1, keepdims=True) + eps)


def depthwise_conv(t, w, pad_left, pad_right):
    return lax.conv_general_dilated(t, w.T[:, None, :].astype(t.dtype), window_strides=(1,),
                                    padding=[(pad_left, pad_right)],
                                    dimension_numbers=('NWC', 'WIO', 'NWC'),
                                    feature_group_count=t.shape[-1])


def axial_rope(n_lat, dim):
    rows = n_lat // GRID_W
    n_freq = dim // 4
    inv = ROPE_BASE ** (-jnp.arange(n_freq, dtype=F32) / n_freq)
    row_pos = jnp.repeat(jnp.arange(rows, dtype=F32), GRID_W)
    col_pos = (jnp.arange(n_lat) % GRID_W).astype(F32)
    ang = jnp.concatenate([row_pos[:, None] * inv, col_pos[:, None] * inv], -1)
    return jnp.cos(ang), jnp.sin(ang)


def apply_rope(t, cos, sin):
    tf = t.reshape(*t.shape[:-1], -1, 2)
    t1, t2 = tf[..., 0], tf[..., 1]
    out = jnp.stack([t1 * cos - t2 * sin, t1 * sin + t2 * cos], -1)
    return out.reshape(t.shape)


def diff_attn_mixer(p_ctx, p_lat, rope_cos, rope_sin, lam_vecs, norm_g, layer_idx, with_ctx):
    G = GROUP_WIDTH

    def split_heads(p):
        B, n, _ = p.shape
        q = p[..., :G].reshape(B, n, N_HEADS, 2, DA_QK)
        k = p[..., G:2 * G].reshape(B, n, N_HEADS, 2, DA_QK)
        return q, k, p[..., 2 * G:]

    qc, kc, vc = split_heads(p_ctx)
    ql, kl, vl = split_heads(p_lat)
    B, n = ql.shape[:2]
    L = qc.shape[1]
    cos = rope_cos[None, :, None, None, :]
    sin = rope_sin[None, :, None, None, :]
    scale = DA_QK ** -0.5
    ql = (apply_rope(ql, cos, sin) * scale).reshape(B, n, G).astype(BF16)
    kl = apply_rope(kl, cos, sin).reshape(B, n, G)
    kc = kc.reshape(B, L, G)
    lam_init = 0.8 - 0.6 * math.exp(-0.3 * layer_idx)
    lv = lam_vecs.astype(F32)
    lam = jnp.exp(jnp.sum(lv[0] * lv[1])) - jnp.exp(jnp.sum(lv[2] * lv[3])) + lam_init
    k_all = jnp.concatenate([kc, kl], 1).astype(BF16)
    v_all = jnp.concatenate([vc, vl], 1).astype(BF16)
    out_l = diff_attention(ql, jnp.swapaxes(k_all, 1, 2), v_all, lam, norm_g, 1.0 - lam_init)
    out_c = None
    if with_ctx:
        out_c = diff_attention((qc * scale).reshape(B, L, G).astype(BF16), jnp.swapaxes(kc.astype(BF16), 1, 2),
                               vc.astype(BF16), lam, norm_g, 1.0 - lam_init)
    return out_c, out_l


def short_conv_mixer(p, conv_w):
    b_g, c_g, xs = jnp.split(p, 3, -1)
    return b_g * depthwise_conv(c_g * xs, conv_w, 1, 1)


def kernel(x, c, ctx, c_ctx, w_ada, b_ada, w_in, dn_conv, dn_a_log, dn_dt_bias, dn_norm, lru_conv, lru_conv_b, lru_wa, lru_ba, lru_wx, lru_bx, lru_lambda, da_lambda, da_norm, sc_conv, w_out, ln1_g, ln1_b, router_w, router_bias, exp_w1, exp_w3, exp_w2, sh_w1, sh_w3, sh_w2, ln2_g, ln2_b):
    depth = w_in.shape[0]
    alpha = (2 * depth) ** 0.25
    B, N, D = x.shape
    L = ctx.shape[1]
    G = GROUP_WIDTH
    silu_c = jax.nn.silu(c)[:, None, :]
    silu_cc = jax.nn.silu(c_ctx)[None, None, :]
    rope_cos, rope_sin = axial_rope(N, DA_QK)
    x = x.reshape(B * N, D)
    xc = ctx.reshape(B * L, D)
    n_gate = 4 * N_HEADS
    for l in range(depth):
        with_ctx = l < depth - 1
        sh1, s1, g1, sh2, s2, g2 = jnp.split(silu_c @ w_ada[l] + b_ada[l], 6, -1)
        sh1c, s1c, g1c, sh2c, s2c, g2c = jnp.split(silu_cc @ w_ada[l] + b_ada[l], 6, -1)
        w_l = w_in[l]
        w_main = jnp.concatenate([w_l[:, :4 * G], w_l[:, 4 * G + n_gate:]], axis=1).astype(BF16)
        w_gate = jnp.pad(w_l[:, 4 * G:4 * G + n_gate], ((0, 0), (0, 128 - n_gate))).astype(BF16)
        pl_all = modulated_matmul(x, s1, sh1, w_main, N).reshape(B, N, -1)
        pc_all = modulated_matmul(xc, s1c, sh1c, w_main, B * L).reshape(B, L, -1)
        gl_all = modulated_matmul(x, s1, sh1, w_gate, N).reshape(B, N, -1)
        gc_all = modulated_matmul(xc, s1c, sh1c, w_gate, B * L).reshape(B, L, -1)
        a_c, a_l = gdn_mixer(pc_all, gc_all, pl_all, gl_all, dn_conv[l], dn_a_log[l], dn_dt_bias[l], dn_norm[l])
        b_c, b_l = rglru_mixer(pc_all, pl_all, 4, 5, lru_conv[l], lru_conv_b[l], lru_wa[l], lru_ba[l],
                               lru_wx[l], lru_bx[l], lru_lambda[l])
        c_c, c_l = diff_attn_mixer(pc_all[..., 6 * G:9 * G], pl_all[..., 6 * G:9 * G], rope_cos, rope_sin,
                                   da_lambda[l], da_norm[l], l, with_ctx)
        d_l = short_conv_mixer(pl_all[..., 9 * G:12 * G], sc_conv[l])
        w_out_l = w_out[l].astype(BF16)
        flat = lambda t: t.reshape(-1, t.shape[-1])
        x_mid, tok_l = out_proj_norm([flat(a_l), flat(b_l), flat(c_l), flat(d_l)], w_out_l, x, g1, s2, sh2,
                                     ln1_g[l], ln1_b[l], alpha, N)
        if with_ctx:
            d_c = short_conv_mixer(pc_all[..., 9 * G:12 * G], sc_conv[l])
            xc_mid, tok_c = out_proj_norm([flat(a_c), flat(b_c), flat(c_c), flat(d_c)], w_out_l, xc, g1c, s2c, sh2c,
                                          ln1_g[l], ln1_b[l], alpha, B * L)
            tok = jnp.concatenate([tok_c, tok_l], 0)
        else:
            tok = tok_l
        y_sorted, dest = moe_routed(tok, router_w[l], router_bias[l], exp_w1[l], exp_w3[l], exp_w2[l])
        shared_w = (sh_w1[l].astype(BF16), sh_w3[l].astype(BF16), sh_w2[l].astype(BF16))
        if with_ctx:
            xc = moe_finish(y_sorted[dest[:B * L].T], tok_c, xc_mid, *shared_w, g2c, ln2_g[l], ln2_b[l], alpha, B * L)
            dest = dest[B * L:]
        x = moe_finish(y_sorted[dest.T], tok_l, x_mid, *shared_w, g2, ln2_g[l], ln2_b[l], alpha, N)
    return x.reshape(B, N, D)
```

```python
import functools
import math

import jax
import jax.numpy as jnp
from jax import lax
from jax.experimental import pallas as pl
from jax.experimental.pallas import tpu as pltpu

F32 = jnp.float32
BF16 = jnp.bfloat16

GRID_W = 64
GROUP_WIDTH = 512
HEAD_DIM = 128
N_HEADS = GROUP_WIDTH // HEAD_DIM
DN_CHUNK = 64
LRU_C = 8.0
DA_QK = HEAD_DIM // 2
ROPE_BASE = 10000.0
N_EXPERTS = 64
TOP_K = 6
ROUTED_SCALE = 2.5

ATTN_TQ = (1024, 512, 256, 128)
ATTN_TK = (1280, 512, 256, 128)
ATTN_UNROLL = 3

V7X_VMEM_BYTES = 64 * 1024 * 1024
VMEM_LIMIT = V7X_VMEM_BYTES * 3 // 4


def _params(n_axes):
    return pltpu.CompilerParams(dimension_semantics=("arbitrary",) * n_axes, vmem_limit_bytes=VMEM_LIMIT)


def _pick_tile(n, candidates):
    for c in candidates:
        if n % c == 0:
            return c
    return n


def _mod_mm_kernel(a_ref, sc_ref, sh_ref, w_ref, o_ref, a_scr):
    @pl.when(pl.program_id(1) == 0)
    def _():
        a_scr[...] = (a_ref[...] * (1.0 + sc_ref[0]) + sh_ref[0]).astype(BF16)

    o_ref[...] = jnp.dot(a_scr[...], w_ref[...], preferred_element_type=F32)


def modulated_matmul(a, scale, shift, w, rows_per_group):
    M, K = a.shape
    N = w.shape[1]
    tm = _pick_tile(rows_per_group, (1024, 512, 256, 128))
    tn = _pick_tile(N, (512, 256, 128))
    tiles_per_group = rows_per_group // tm
    grp = lambda i, j: (i // tiles_per_group, 0, 0)
    return pl.pallas_call(
        _mod_mm_kernel,
        grid=(M // tm, N // tn),
        in_specs=[pl.BlockSpec((tm, K), lambda i, j: (i, 0)),
                  pl.BlockSpec((1, 1, K), grp),
                  pl.BlockSpec((1, 1, K), grp),
                  pl.BlockSpec((K, tn), lambda i, j: (0, j))],
        out_specs=pl.BlockSpec((tm, tn), lambda i, j: (i, j)),
        out_shape=jax.ShapeDtypeStruct((M, N), F32),
        scratch_shapes=[pltpu.VMEM((tm, K), BF16)],
        compiler_params=_params(2),
        name="modulated_matmul",
    )(a, scale, shift, w)


def _layer_norm_rows(r, g, b, eps):
    mu = jnp.mean(r, axis=-1, keepdims=True)
    d = r - mu
    return d * lax.rsqrt(jnp.mean(d * d, axis=-1, keepdims=True) + eps) * g + b


def _out_proj_kernel(a_ref, b_ref, c_ref, d_ref, w_ref, x_ref, g1_ref, s2_ref, sh2_ref, lg_ref, lb_ref,
                     xmid_ref, tok_ref, *, alpha):
    G = GROUP_WIDTH
    mix = sum(jnp.dot(r[...].astype(BF16), w_ref[g * G:(g + 1) * G, :], preferred_element_type=F32)
              for g, r in enumerate((a_ref, b_ref, c_ref, d_ref)))
    x_mid = _layer_norm_rows(alpha * x_ref[...] + g1_ref[0] * mix, lg_ref[...], lb_ref[...], 1e-5)
    xmid_ref[...] = x_mid
    tok_ref[...] = x_mid * (1.0 + s2_ref[0]) + sh2_ref[0]


def out_proj_norm(mix_parts, w, x, g1, s2, sh2, ln_g, ln_b, alpha, rows_per_group):
    M, D = x.shape
    G = GROUP_WIDTH
    tm = _pick_tile(rows_per_group, (256, 128))
    tiles_per_group = rows_per_group // tm
    grp = lambda i: (i // tiles_per_group, 0, 0)
    row = lambda width: pl.BlockSpec((tm, width), lambda i: (i, 0))
    vec = pl.BlockSpec((1, D), lambda i: (0, 0))
    return pl.pallas_call(
        functools.partial(_out_proj_kernel, alpha=alpha),
        grid=(M // tm,),
        in_specs=[row(G), row(G), row(G), row(G), pl.BlockSpec((D, D), lambda i: (0, 0)), row(D),
                  pl.BlockSpec((1, 1, D), grp), pl.BlockSpec((1, 1, D), grp), pl.BlockSpec((1, 1, D), grp), vec, vec],
        out_specs=[row(D), row(D)],
        out_shape=[jax.ShapeDtypeStruct((M, D), F32), jax.ShapeDtypeStruct((M, D), F32)],
        compiler_params=_params(1),
        name="out_proj_norm",
    )(*mix_parts, w, x, g1, s2, sh2, ln_g.reshape(1, D), ln_b.reshape(1, D))


def _attn_kernel(lam_ref, g_ref, q_ref, kt_ref, v_ref, o_ref, m_scr, acc_scr, *, tk, n_chunks, out_scale):
    q = q_ref[0]
    lane = lax.broadcasted_iota(jnp.int32, q.shape, 1)
    zero = jnp.zeros_like(q)
    q_maps = (jnp.where(lane < DA_QK, q, zero), jnp.where(lane >= DA_QK, q, zero))
    m_scr[...] = jnp.full(m_scr.shape, -jnp.inf, F32)
    acc_scr[...] = jnp.zeros(acc_scr.shape, F32)
    ones = jnp.ones((tk, HEAD_DIM), BF16)

    def chunk(c, carry):
        off = pl.multiple_of(c * tk, tk)
        ktc = kt_ref[0, :, pl.ds(off, tk)]
        v_ext = jnp.concatenate([v_ref[0, pl.ds(off, tk), :], ones], axis=1)
        for mp in range(2):
            s = jnp.dot(q_maps[mp], ktc, preferred_element_type=F32)
            m_prev = m_scr[mp]
            m_new = jnp.maximum(m_prev, jnp.max(s, axis=-1, keepdims=True))
            p = jnp.exp(s - pltpu.repeat(m_new, tk // HEAD_DIM, axis=1))
            alpha = jnp.exp(m_prev - m_new)
            acc_scr[mp] = (pltpu.repeat(alpha, 2, axis=1) * acc_scr[mp]
                           + jnp.dot(p.astype(BF16), v_ext, preferred_element_type=F32))
            m_scr[mp] = m_new
        return carry

    lax.fori_loop(0, n_chunks, chunk, 0, unroll=ATTN_UNROLL)
    a0, a1 = acc_scr[0], acc_scr[1]
    o = a0[:, :HEAD_DIM] / a0[:, HEAD_DIM:] - lam_ref[...] * (a1[:, :HEAD_DIM] / a1[:, HEAD_DIM:])
    o = o * lax.rsqrt(jnp.mean(o * o, axis=-1, keepdims=True) + 1e-6)
    o_ref[0] = o * g_ref[...] * out_scale


def diff_attention(q, kt, v, lam, norm_g, out_scale):
    B, n, G = q.shape
    m = v.shape[1]
    tq = _pick_tile(n, ATTN_TQ)
    tk = _pick_tile(m, ATTN_TK)
    kern = functools.partial(_attn_kernel, tk=tk, n_chunks=m // tk, out_scale=out_scale)
    lam_row = jnp.full((1, HEAD_DIM), lam, F32)
    return pl.pallas_call(
        kern,
        grid=(B, N_HEADS, n // tq),
        in_specs=[pl.BlockSpec((1, HEAD_DIM), lambda b, h, i: (0, 0)),
                  pl.BlockSpec((1, HEAD_DIM), lambda b, h, i: (0, 0)),
                  pl.BlockSpec((1, tq, HEAD_DIM), lambda b, h, i: (b, i, h)),
                  pl.BlockSpec((1, HEAD_DIM, m), lambda b, h, i: (b, h, 0)),
                  pl.BlockSpec((1, m, HEAD_DIM), lambda b, h, i: (b, 0, h))],
        out_specs=pl.BlockSpec((1, tq, HEAD_DIM), lambda b, h, i: (b, i, h)),
        out_shape=jax.ShapeDtypeStruct((B, n, G), F32),
        scratch_shapes=[pltpu.VMEM((2, tq, HEAD_DIM), F32), pltpu.VMEM((2, tq, 2 * HEAD_DIM), F32)],
        compiler_params=_params(3),
        name="diff_attention",
    )(lam_row, norm_g.reshape(1, HEAD_DIM).astype(F32), q, kt, v)


def _lru_kernel(h0_ref, xprev_ref, x_ref, xnext_ref, cw_ref, cb_ref, wg_ref, bg_ref, sp_ref, *rest,
                tb, n_blocks, reverse, finish):
    if finish:
        hother_ref, gate_ref, out_ref, state_ref, a_scr, b_scr, h_scr = rest
    else:
        out_ref, state_ref, a_scr, b_scr, h_scr = rest
    j = pl.program_id(1)
    t_blk = (n_blocks - 1 - j) if reverse else j
    W = x_ref.shape[-1]

    @pl.when(j == 0)
    def _():
        h_scr[...] = jnp.broadcast_to(h0_ref[0], h_scr.shape)

    prev = jnp.where(t_blk > 0, xprev_ref[0], 0.0)
    nxt = jnp.where(t_blk < n_blocks - 1, xnext_ref[0], 0.0)
    x_ext = jnp.concatenate([prev, x_ref[0], nxt], axis=0)
    cw = cw_ref[...]
    xb = cb_ref[...] + sum(cw[k:k + 1] * x_ext[6 + k:6 + k + tb] for k in range(4))
    xb16 = xb.astype(BF16)
    hw = W // N_HEADS
    pre = [jnp.dot(xb16[:, h * hw:(h + 1) * hw], wg_ref[h], preferred_element_type=F32) for h in range(N_HEADS)]
    pre_r = jnp.concatenate([p[:, :hw] for p in pre], axis=1) + bg_ref[0:1]
    pre_i = jnp.concatenate([p[:, hw:] for p in pre], axis=1) + bg_ref[1:2]
    log_a = -LRU_C * jax.nn.sigmoid(pre_r) * sp_ref[...]
    a = jnp.exp(log_a)
    a_scr[...] = a
    b_scr[...] = jnp.sqrt(-jnp.tanh(log_a) * (a * a + 1.0)) * (jax.nn.sigmoid(pre_i) * xb)

    row = lax.broadcasted_iota(jnp.int32, (8, W), 0)
    n_rows = tb // 8

    def block(i, h):
        blk = (n_rows - 1 - i) if reverse else i
        r0 = pl.multiple_of(blk * 8, 8)
        a = a_scr[pl.ds(r0, 8), :]
        b = b_scr[pl.ds(r0, 8), :]
        for d in (1, 2, 4):
            if reverse:
                keep = row < 8 - d
                a_sh, b_sh = pltpu.roll(a, 8 - d, 0), pltpu.roll(b, 8 - d, 0)
            else:
                keep = row >= d
                a_sh, b_sh = pltpu.roll(a, d, 0), pltpu.roll(b, d, 0)
            b = jnp.where(keep, b + a * b_sh, b)
            a = jnp.where(keep, a * a_sh, a)
        hb = b + a * h
        b_scr[pl.ds(r0, 8), :] = hb
        last = hb[0:1] if reverse else hb[7:8]
        return jnp.broadcast_to(last, (8, W))

    h_fin = lax.fori_loop(0, n_rows, block, h_scr[...], unroll=4)
    h_scr[...] = h_fin
    state_ref[0] = h_fin
    if finish:
        out_ref[0] = (hother_ref[0] + b_scr[...]) * jax.nn.gelu(gate_ref[0])
    else:
        out_ref[0] = b_scr[...]


def lru_scan(p, x_col, gate_col, h0, cw, cb, wg, bg, sp, reverse, h_other=None):
    B, n, _ = p.shape
    W = GROUP_WIDTH
    tb = _pick_tile(n, (512, 256, 128))
    nb = n // tb
    finish = h_other is not None
    tix = (lambda j: nb - 1 - j) if reverse else (lambda j: j)
    r8 = tb // 8
    in_specs = [pl.BlockSpec((1, 8, W), lambda b, j: (b, 0, 0)),
                pl.BlockSpec((1, 8, W), lambda b, j: (b, jnp.maximum(tix(j) * r8 - 1, 0), x_col)),
                pl.BlockSpec((1, tb, W), lambda b, j: (b, tix(j), x_col)),
                pl.BlockSpec((1, 8, W), lambda b, j: (b, jnp.minimum((tix(j) + 1) * r8, n // 8 - 1), x_col)),
                pl.BlockSpec((4, W), lambda b, j: (0, 0)),
                pl.BlockSpec((1, W), lambda b, j: (0, 0)),
                pl.BlockSpec((N_HEADS, W // N_HEADS, 2 * W // N_HEADS), lambda b, j: (0, 0, 0)),
                pl.BlockSpec((2, W), lambda b, j: (0, 0)),
                pl.BlockSpec((1, W), lambda b, j: (0, 0))]
    args = [h0, p, p, p, cw, cb, wg, bg, sp]
    if finish:
        in_specs += [pl.BlockSpec((1, tb, W), lambda b, j: (b, tix(j), 0)),
                     pl.BlockSpec((1, tb, W), lambda b, j: (b, tix(j), gate_col))]
        args += [h_other, p]
    kern = functools.partial(_lru_kernel, tb=tb, n_blocks=nb, reverse=reverse, finish=finish)
    return pl.pallas_call(
        kern,
        grid=(B, nb),
        in_specs=in_specs,
        out_specs=[pl.BlockSpec((1, tb, W), lambda b, j: (b, tix(j), 0)),
                   pl.BlockSpec((1, 8, W), lambda b, j: (b, 0, 0))],
        out_shape=[jax.ShapeDtypeStruct((B, n, W), F32), jax.ShapeDtypeStruct((B, 8, W), F32)],
        scratch_shapes=[pltpu.VMEM((tb, W), F32), pltpu.VMEM((tb, W), F32), pltpu.VMEM((8, W), F32)],
        compiler_params=_params(2),
        name="lru_scan",
    )(*args)


def rglru_mixer(p_ctx, p_lat, x_col, gate_col, conv_w, conv_b, wa, ba, wx, bx, lam):
    B = p_lat.shape[0]
    W = GROUP_WIDTH
    cw = conv_w.T.astype(F32)
    cb = conv_b.reshape(1, W).astype(F32)
    h0 = jnp.zeros((B, 8, W), F32)
    outs = []
    states = {}
    for d, reverse in ((0, False), (1, True)):
        wg = jnp.concatenate([wa[d], wx[d]], axis=-1).astype(BF16)
        bg = jnp.stack([ba[d], bx[d]]).astype(F32)
        sp = jax.nn.softplus(-lam[d]).reshape(1, W).astype(F32)
        states[d] = (wg, bg, sp)
    wg, bg, sp = states[0]
    hc_f, sc_f = lru_scan(p_ctx, x_col, gate_col, h0, cw, cb, wg, bg, sp, False)
    hl_f, _ = lru_scan(p_lat, x_col, gate_col, sc_f, cw, cb, wg, bg, sp, False)
    wg, bg, sp = states[1]
    out_c, sc_b = lru_scan(p_ctx, x_col, gate_col, h0, cw, cb, wg, bg, sp, True, h_other=hc_f)
    out_l, _ = lru_scan(p_lat, x_col, gate_col, sc_b, cw, cb, wg, bg, sp, True, h_other=hl_f)
    return out_c, out_l


def _split_bf16(a):
    hi = a.astype(BF16)
    return hi, (a - hi.astype(F32)).astype(BF16)


def _dot3(a, b):
    ah, al = _split_bf16(a)
    bh, bl = _split_bf16(b)
    d = functools.partial(jnp.dot, preferred_element_type=F32)
    return d(ah, bh) + (d(ah, bl) + d(al, bh))


def _dot_exact_lhs(a_exact, b):
    bh, bl = _split_bf16(b)
    d = functools.partial(jnp.dot, preferred_element_type=F32)
    return d(a_exact, bh) + d(a_exact, bl)


def _gdn_kernel(s0_ref, xprev_ref, x_ref, xnext_ref, gate_ref, cw_ref, ea_ref, dt_ref, ng_ref, *rest,
                tb, n_blocks, reverse, finish, direction):
    if finish:
        oother_ref, z_ref, out_ref, state_ref, s_scr = rest
    else:
        out_ref, state_ref, s_scr = rest
    C = DN_CHUNK
    HD = HEAD_DIM
    j = pl.program_id(1)
    t_blk = (n_blocks - 1 - j) if reverse else j

    @pl.when(j == 0)
    def _():
        s_scr[...] = s0_ref[0]

    prev = jnp.where(t_blk > 0, xprev_ref[0], 0.0)
    nxt = jnp.where(t_blk < n_blocks - 1, xnext_ref[0], 0.0)
    x_ext = jnp.concatenate([prev, x_ref[0], nxt], axis=0)
    cw = cw_ref[...]
    qkv = sum(cw[k:k + 1] * x_ext[6 + k:6 + k + tb] for k in range(4))
    qkv = qkv * jax.nn.sigmoid(qkv)

    gl_in = gate_ref[0]
    g_log = -ea_ref[...] * jax.nn.softplus(gl_in + dt_ref[...])
    beta_all = jax.nn.sigmoid(gl_in)
    r_i = lax.broadcasted_iota(jnp.int32, (tb, tb), 0)
    c_i = lax.broadcasted_iota(jnp.int32, (tb, tb), 1)
    same = (r_i // C) == (c_i // C)
    incl = same & ((c_i >= r_i) if reverse else (c_i <= r_i))
    strict = same & ((c_i > r_i) if reverse else (c_i < r_i))
    ones_where = lambda msk: jnp.where(msk, 1.0, 0.0).astype(BF16)
    g_cum = _dot_exact_lhs(ones_where(incl), g_log)
    g_tot = _dot_exact_lhs(ones_where(same), g_log)
    e_cum = jnp.exp(g_cum)
    e_rest = jnp.exp(g_tot - g_cum)
    e_tot = jnp.exp(g_tot)
    g_cum_t = jnp.transpose(g_cum)
    eye = jnp.where(r_i == c_i, 1.0, 0.0)
    dt = (((1,), (1,)), ((), ()))

    s_cur = [s_scr[h] for h in range(N_HEADS)]
    if finish:
        o_other, z_all = oother_ref[0], z_ref[0]
    heads = range(N_HEADS)
    lane_of = [direction * 8 + h for h in heads]
    col = lambda a, lane: jnp.broadcast_to(a[:, lane:lane + 1], (tb, HD))
    qs, ks, vbs, kbs, attns, ys, ts = [], [], [], [], [], [], []
    for h in heads:
        q = qkv[:, h * HD:(h + 1) * HD]
        k = qkv[:, GROUP_WIDTH + h * HD:GROUP_WIDTH + (h + 1) * HD]
        v = qkv[:, 2 * GROUP_WIDTH + h * HD:2 * GROUP_WIDTH + (h + 1) * HD]
        q = q * lax.rsqrt(jnp.sum(q * q, axis=-1, keepdims=True) + 1e-6) * HD ** -0.5
        k = k * lax.rsqrt(jnp.sum(k * k, axis=-1, keepdims=True) + 1e-6)
        beta = col(beta_all, lane_of[h] + 4)
        kb = k * beta
        k16 = k.astype(BF16)
        decay = jnp.exp(jnp.where(incl, col(g_cum, lane_of[h])[:, :tb] - g_cum_t[lane_of[h]:lane_of[h] + 1, :],
                                  -jnp.inf))
        m_low = jnp.where(strict, lax.dot_general(kb.astype(BF16), k16, dt, preferred_element_type=F32) * decay, 0.0)
        attns.append((lax.dot_general(q.astype(BF16), k16, dt, preferred_element_type=F32) * decay).astype(BF16))
        qs.append(q)
        ks.append(k)
        vbs.append(v * beta)
        kbs.append(kb)
        ys.append(-m_low)
        ts.append(eye - m_low)
    for _ in range(5):
        ys = [_dot3(y, y) for y in ys]
        ts = [t + _dot3(t, y) for t, y in zip(ts, ys)]
    us, ws, k_decs, q_decs = [], [], [], []
    for h in heads:
        rhs = jnp.concatenate([vbs[h], kbs[h] * col(e_cum, lane_of[h])], axis=1).astype(BF16)
        uw = jnp.dot(ts[h].astype(BF16), rhs, preferred_element_type=F32)
        us.append(uw[:, :HD])
        ws.append(uw[:, HD:].astype(BF16))
        k_decs.append((ks[h] * col(e_rest, lane_of[h])).astype(BF16))
        q_decs.append((qs[h] * col(e_cum, lane_of[h])).astype(BF16))
    n_ch = tb // C
    v_new = [[None] * n_ch for _ in heads]
    o_inter = [[None] * n_ch for _ in heads]
    for ci in (range(n_ch - 1, -1, -1) if reverse else range(n_ch)):
        rows = slice(ci * C, (ci + 1) * C)
        for h in heads:
            s16 = s_cur[h].astype(BF16)
            v_new[h][ci] = us[h][rows] - jnp.dot(ws[h][rows], s16, preferred_element_type=F32)
            o_inter[h][ci] = jnp.dot(q_decs[h][rows], s16, preferred_element_type=F32)
        for h in heads:
            s_cur[h] = s_cur[h] * col(e_tot, lane_of[h])[ci * C:ci * C + 1] + lax.dot_general(
                k_decs[h][rows], v_new[h][ci].astype(BF16), (((0,), (0,)), ((), ())), preferred_element_type=F32)
    outs = []
    for h in heads:
        v_all = jnp.concatenate(v_new[h], axis=0).astype(BF16)
        o = jnp.concatenate(o_inter[h], axis=0) + jnp.dot(attns[h], v_all, preferred_element_type=F32)
        if finish:
            o = o + o_other[:, h * HD:(h + 1) * HD]
            o = o * lax.rsqrt(jnp.mean(o * o, axis=-1, keepdims=True) + 1e-6) * ng_ref[...]
            z = z_all[:, h * HD:(h + 1) * HD]
            o = o * (z * jax.nn.sigmoid(z))
        outs.append(o)
    out_ref[0] = jnp.concatenate(outs, axis=1)
    s_fin = jnp.stack(s_cur)
    s_scr[...] = s_fin
    state_ref[0] = s_fin


def gdn_scan(p, pg, s0, cw, ea, dt, ng, direction, o_other=None):
    B, n, _ = p.shape
    G = GROUP_WIDTH
    reverse = direction == 1
    tb = _pick_tile(n, (128,))
    nb = n // tb
    finish = o_other is not None
    tix = (lambda j: nb - 1 - j) if reverse else (lambda j: j)
    r8 = tb // 8
    vec = lambda w: pl.BlockSpec((1, w), lambda b, j: (0, 0))
    in_specs = [pl.BlockSpec((1, N_HEADS, HEAD_DIM, HEAD_DIM), lambda b, j: (b, 0, 0, 0)),
                pl.BlockSpec((1, 8, 3 * G), lambda b, j: (b, jnp.maximum(tix(j) * r8 - 1, 0), 0)),
                pl.BlockSpec((1, tb, 3 * G), lambda b, j: (b, tix(j), 0)),
                pl.BlockSpec((1, 8, 3 * G), lambda b, j: (b, jnp.minimum((tix(j) + 1) * r8, n // 8 - 1), 0)),
                pl.BlockSpec((1, tb, 128), lambda b, j: (b, tix(j), 0)),
                pl.BlockSpec((4, 3 * G), lambda b, j: (0, 0)),
                vec(128), vec(128), vec(HEAD_DIM)]
    args = [s0, p, p, p, pg, cw, ea, dt, ng]
    if finish:
        in_specs += [pl.BlockSpec((1, tb, G), lambda b, j: (b, tix(j), 0)),
                     pl.BlockSpec((1, tb, G), lambda b, j: (b, tix(j), 3))]
        args += [o_other, p]
    kern = functools.partial(_gdn_kernel, tb=tb, n_blocks=nb, reverse=reverse, finish=finish, direction=direction)
    return pl.pallas_call(
        kern,
        grid=(B, nb),
        in_specs=in_specs,
        out_specs=[pl.BlockSpec((1, tb, G), lambda b, j: (b, tix(j), 0)),
                   pl.BlockSpec((1, N_HEADS, HEAD_DIM, HEAD_DIM), lambda b, j: (b, 0, 0, 0))],
        out_shape=[jax.ShapeDtypeStruct((B, n, G), F32),
                   jax.ShapeDtypeStruct((B, N_HEADS, HEAD_DIM, HEAD_DIM), F32)],
        scratch_shapes=[pltpu.VMEM((N_HEADS, HEAD_DIM, HEAD_DIM), F32)],
        compiler_params=_params(2),
        name="gdn_scan",
    )(*args)


def gdn_mixer(p_ctx, g_ctx, p_lat, g_lat, conv_w, a_log, dt_bias, norm_g):
    B = p_lat.shape[0]
    cw = conv_w.T.astype(F32)
    zeros4 = jnp.zeros((2, N_HEADS), F32)
    lanes = lambda t: jnp.pad(jnp.stack([t, zeros4], axis=1).reshape(1, -1), ((0, 0), (0, 128 - 4 * N_HEADS)))
    ea = lanes(jnp.exp(a_log.astype(F32)))
    dt = lanes(dt_bias.astype(F32))
    ng = norm_g.reshape(1, HEAD_DIM).astype(F32)
    s0 = jnp.zeros((B, N_HEADS, HEAD_DIM, HEAD_DIM), F32)
    oc_f, sc_f = gdn_scan(p_ctx, g_ctx, s0, cw, ea, dt, ng, 0)
    ol_f, _ = gdn_scan(p_lat, g_lat, sc_f, cw, ea, dt, ng, 0)
    out_c, sc_b = gdn_scan(p_ctx, g_ctx, s0, cw, ea, dt, ng, 1, o_other=oc_f)
    out_l, _ = gdn_scan(p_lat, g_lat, sc_b, cw, ea, dt, ng, 1, o_other=ol_f)
    return out_c, out_l


def _moe_kernel(be_ref, nused_ref, x_ref, w1_ref, w3_ref, w2_ref, wr_ref, y_ref, w1_scr, w3_scr, w2_scr):
    i = pl.program_id(0)

    @pl.when((i == 0) | (be_ref[i] != be_ref[jnp.maximum(i - 1, 0)]))
    def _():
        w1_scr[...] = w1_ref[0, 0].astype(BF16)
        w3_scr[...] = w3_ref[0, 0].astype(BF16)
        w2_scr[...] = w2_ref[0, 0].astype(BF16)

    @pl.when(i < nused_ref[0])
    def _():
        x = x_ref[...].astype(BF16)
        h1 = jnp.dot(x, w1_scr[...], preferred_element_type=F32)
        h3 = jnp.dot(x, w3_scr[...], preferred_element_type=F32)
        hid = (h1 * jax.nn.sigmoid(h1) * h3).astype(BF16)
        y_ref[...] = jnp.dot(hid, w2_scr[...], preferred_element_type=F32) * wr_ref[...]

    @pl.when(i >= nused_ref[0])
    def _():
        y_ref[...] = jnp.zeros(y_ref.shape, y_ref.dtype)


def grouped_experts(x_sorted, blk_e, n_used, wr, w1, w3, w2, layer, tm):
    R, D = x_sorted.shape
    FF = w1.shape[-1]
    grid_spec = pltpu.PrefetchScalarGridSpec(
        num_scalar_prefetch=2,
        grid=(R // tm,),
        in_specs=[pl.BlockSpec((tm, D), lambda i, be, nu: (i, 0)),
                  pl.BlockSpec((1, 1, D, FF), lambda i, be, nu: (layer, be[i], 0, 0)),
                  pl.BlockSpec((1, 1, D, FF), lambda i, be, nu: (layer, be[i], 0, 0)),
                  pl.BlockSpec((1, 1, FF, D), lambda i, be, nu: (layer, be[i], 0, 0)),
                  pl.BlockSpec((tm, 1), lambda i, be, nu: (i, 0))],
        out_specs=pl.BlockSpec((tm, D), lambda i, be, nu: (i, 0)),
        scratch_shapes=[pltpu.VMEM((D, FF), BF16), pltpu.VMEM((D, FF), BF16), pltpu.VMEM((FF, D), BF16)],
    )
    return pl.pallas_call(
        _moe_kernel,
        grid_spec=grid_spec,
        out_shape=jax.ShapeDtypeStruct((R, D), F32),
        compiler_params=_params(1),
        name="grouped_experts",
    )(blk_e, n_used, x_sorted, w1, w3, w2, wr)


def _moe_finish_kernel(yg_ref, tok_ref, xmid_ref, w1_ref, w3_ref, w2_ref, g2_ref, lg_ref, lb_ref, o_ref, *, alpha):
    x = tok_ref[...].astype(BF16)
    h1 = jnp.dot(x, w1_ref[...], preferred_element_type=F32)
    h3 = jnp.dot(x, w3_ref[...], preferred_element_type=F32)
    hid = (h1 * jax.nn.sigmoid(h1) * h3).astype(BF16)
    y = jnp.sum(yg_ref[...], axis=0) + jnp.dot(hid, w2_ref[...], preferred_element_type=F32)
    o_ref[...] = _layer_norm_rows(alpha * xmid_ref[...] + g2_ref[0] * y, lg_ref[...], lb_ref[...], 1e-5)


def moe_finish(yg, tok, x_mid, sw1, sw3, sw2, g2, ln_g, ln_b, alpha, rows_per_group):
    M, D = tok.shape
    FF = sw1.shape[-1]
    tm = _pick_tile(rows_per_group, (128,))
    tiles_per_group = rows_per_group // tm
    row = pl.BlockSpec((tm, D), lambda i: (i, 0))
    vec = pl.BlockSpec((1, D), lambda i: (0, 0))
    return pl.pallas_call(
        functools.partial(_moe_finish_kernel, alpha=alpha),
        grid=(M // tm,),
        in_specs=[pl.BlockSpec((TOP_K, tm, D), lambda i: (0, i, 0)), row, row,
                  pl.BlockSpec((D, FF), lambda i: (0, 0)),
                  pl.BlockSpec((D, FF), lambda i: (0, 0)),
                  pl.BlockSpec((FF, D), lambda i: (0, 0)),
                  pl.BlockSpec((1, 1, D), lambda i: (i // tiles_per_group, 0, 0)), vec, vec],
        out_specs=row,
        out_shape=jax.ShapeDtypeStruct((M, D), F32),
        compiler_params=_params(1),
        name="moe_finish",
    )(yg, tok, x_mid, sw1, sw3, sw2, g2, ln_g.reshape(1, D), ln_b.reshape(1, D))


def moe_routed(tok, router_w, router_bias, w1, w3, w2, layer):
    T, D = tok.shape
    tm = 512 if T * TOP_K >= 64 * 512 else 128
    scores = jax.nn.sigmoid(jnp.matmul(tok, router_w, preferred_element_type=F32))
    _, idx = lax.top_k(scores + router_bias.astype(F32), TOP_K)
    wts = jnp.take_along_axis(scores, idx, -1)
    wts = wts / jnp.sum(wts, -1, keepdims=True) * ROUTED_SCALE
    A = T * TOP_K
    flat_e = idx.reshape(-1).astype(jnp.int32)
    aid = jnp.arange(A, dtype=jnp.int32)
    _, order, w_sorted = lax.sort((flat_e * A + aid, aid, wts.reshape(-1)), num_keys=1)
    counts = jnp.sum((flat_e[:, None] == jnp.arange(N_EXPERTS, dtype=jnp.int32)[None, :]).astype(jnp.int32), axis=0)
    ends = jnp.cumsum(counts)
    starts = ends - counts
    padded = (counts + tm - 1) // tm * tm
    pad_ends = jnp.cumsum(padded)
    pad_starts = pad_ends - padded
    gap_before = pad_starts - starts
    gap_step = jnp.diff(gap_before, prepend=0)
    shift = jnp.sum(jnp.where(aid[:, None] >= starts[None, :], gap_step[None, :], 0), axis=1)
    _, dest = lax.sort((order, aid + shift), num_keys=1)
    n_blocks = -(-A // tm) + N_EXPERTS
    R = n_blocks * tm
    blk_e = jnp.minimum(jnp.sum((jnp.arange(n_blocks, dtype=jnp.int32)[:, None] * tm >= pad_ends[None, :])
                                .astype(jnp.int32), axis=1), N_EXPERTS - 1)
    n_used = (pad_ends[-1] // tm).astype(jnp.int32).reshape(1)
    row = jnp.arange(R, dtype=jnp.int32).reshape(n_blocks, tm)
    src = row - gap_before[blk_e][:, None]
    valid = (row < (pad_starts + counts)[blk_e][:, None]) & (row < pad_ends[-1])
    src = jnp.where(valid, src, 0).reshape(R)
    buf_tok = jnp.where(valid.reshape(R), order[src] // TOP_K, 0)
    buf_w = jnp.where(valid.reshape(R), w_sorted[src], 0.0)
    y_sorted = grouped_experts(tok[buf_tok], blk_e, n_used, buf_w.reshape(R, 1), w1, w3, w2, layer, tm)
    return y_sorted, dest.reshape(T, TOP_K)


def layer_norm(t, g, b, eps=1e-5):
    mu = jnp.mean(t, -1, keepdims=True)
    var = jnp.mean(jnp.square(t - mu), -1, keepdims=True)
    return (t - mu) * lax.rsqrt(var + eps) * g + b


def rms_norm(t, g, eps=1e-6):
    return t * lax.rsqrt(jnp.mean(jnp.square(t), -1, keepdims=True) + eps) * g


def l2norm(t, eps=1e-6):
    return t * lax.rsqrt(jnp.sum(jnp.square(t), -1, keepdims=True) + eps)


def depthwise_conv(t, w, pad_left, pad_right):
    return lax.conv_general_dilated(t, w.T[:, None, :].astype(t.dtype), window_strides=(1,),
                                    padding=[(pad_left, pad_right)],
                                    dimension_numbers=('NWC', 'WIO', 'NWC'),
                                    feature_group_count=t.shape[-1])


def axial_rope(n_lat, dim):
    rows = n_lat // GRID_W
    n_freq = dim // 4
    inv = ROPE_BASE ** (-jnp.arange(n_freq, dtype=F32) / n_freq)
    row_pos = jnp.repeat(jnp.arange(rows, dtype=F32), GRID_W)
    col_pos = (jnp.arange(n_lat) % GRID_W).astype(F32)
    ang = jnp.concatenate([row_pos[:, None] * inv, col_pos[:, None] * inv], -1)
    return jnp.cos(ang), jnp.sin(ang)


def apply_rope(t, cos, sin):
    tf = t.reshape(*t.shape[:-1], -1, 2)
    t1, t2 = tf[..., 0], tf[..., 1]
    out = jnp.stack([t1 * cos - t2 * sin, t1 * sin + t2 * cos], -1)
    return out.reshape(t.shape)


def diff_attn_mixer(p_ctx, p_lat, rope_cos, rope_sin, lam_vecs, norm_g, layer_idx, with_ctx):
    G = GROUP_WIDTH

    def split_heads(p):
        B, n, _ = p.shape
        q = p[..., :G].reshape(B, n, N_HEADS, 2, DA_QK)
        k = p[..., G:2 * G].reshape(B, n, N_HEADS, 2, DA_QK)
        return q, k, p[..., 2 * G:]

    qc, kc, vc = split_heads(p_ctx)
    ql, kl, vl = split_heads(p_lat)
    B, n = ql.shape[:2]
    L = qc.shape[1]
    cos = rope_cos[None, :, None, None, :]
    sin = rope_sin[None, :, None, None, :]
    scale = DA_QK ** -0.5
    ql = (apply_rope(ql, cos, sin) * scale).reshape(B, n, G).astype(BF16)
    kl = apply_rope(kl, cos, sin).reshape(B, n, G)
    kc = kc.reshape(B, L, G)
    lam_init = 0.8 - 0.6 * math.exp(-0.3 * layer_idx)
    lv = lam_vecs.astype(F32)
    lam = jnp.exp(jnp.sum(lv[0] * lv[1])) - jnp.exp(jnp.sum(lv[2] * lv[3])) + lam_init
    k_all = jnp.concatenate([kc, kl], 1).astype(BF16)
    v_all = jnp.concatenate([vc, vl], 1).astype(BF16)
    out_l = diff_attention(ql, jnp.swapaxes(k_all, 1, 2), v_all, lam, norm_g, 1.0 - lam_init)
    out_c = None
    if with_ctx:
        out_c = diff_attention((qc * scale).reshape(B, L, G).astype(BF16), jnp.swapaxes(kc.astype(BF16), 1, 2),
                               vc.astype(BF16), lam, norm_g, 1.0 - lam_init)
    return out_c, out_l


def short_conv_mixer(p, conv_w):
    b_g, c_g, xs = jnp.split(p, 3, -1)
    return b_g * depthwise_conv(c_g * xs, conv_w, 1, 1)


def kernel(x, c, ctx, c_ctx, w_ada, b_ada, w_in, dn_conv, dn_a_log, dn_dt_bias, dn_norm, lru_conv, lru_conv_b, lru_wa, lru_ba, lru_wx, lru_bx, lru_lambda, da_lambda, da_norm, sc_conv, w_out, ln1_g, ln1_b, router_w, router_bias, exp_w1, exp_w3, exp_w2, sh_w1, sh_w3, sh_w2, ln2_g, ln2_b):
    depth = w_in.shape[0]
    alpha = (2 * depth) ** 0.25
    B, N, D = x.shape
    L = ctx.shape[1]
    G = GROUP_WIDTH
    silu_c = jax.nn.silu(c)[:, None, :]
    silu_cc = jax.nn.silu(c_ctx)[None, None, :]
    rope_cos, rope_sin = axial_rope(N, DA_QK)
    x = x.reshape(B * N, D)
    xc = ctx.reshape(B * L, D)
    n_gate = 4 * N_HEADS
    for l in range(depth):
        with_ctx = l < depth - 1
        sh1, s1, g1, sh2, s2, g2 = jnp.split(silu_c @ w_ada[l] + b_ada[l], 6, -1)
        sh1c, s1c, g1c, sh2c, s2c, g2c = jnp.split(silu_cc @ w_ada[l] + b_ada[l], 6, -1)
        w_l = w_in[l]
        w_main = jnp.concatenate([w_l[:, :4 * G], w_l[:, 4 * G + n_gate:]], axis=1).astype(BF16)
        w_gate = jnp.pad(w_l[:, 4 * G:4 * G + n_gate], ((0, 0), (0, 128 - n_gate))).astype(BF16)
        pl_all = modulated_matmul(x, s1, sh1, w_main, N).reshape(B, N, -1)
        pc_all = modulated_matmul(xc, s1c, sh1c, w_main, B * L).reshape(B, L, -1)
        gl_all = modulated_matmul(x, s1, sh1, w_gate, N).reshape(B, N, -1)
        gc_all = modulated_matmul(xc, s1c, sh1c, w_gate, B * L).reshape(B, L, -1)
        a_c, a_l = gdn_mixer(pc_all, gc_all, pl_all, gl_all, dn_conv[l], dn_a_log[l], dn_dt_bias[l], dn_norm[l])
        b_c, b_l = rglru_mixer(pc_all, pl_all, 4, 5, lru_conv[l], lru_conv_b[l], lru_wa[l], lru_ba[l],
                               lru_wx[l], lru_bx[l], lru_lambda[l])
        c_c, c_l = diff_attn_mixer(pc_all[..., 6 * G:9 * G], pl_all[..., 6 * G:9 * G], rope_cos, rope_sin,
                                   da_lambda[l], da_norm[l], l, with_ctx)
        d_l = short_conv_mixer(pl_all[..., 9 * G:12 * G], sc_conv[l])
        w_out_l = w_out[l].astype(BF16)
        flat = lambda t: t.reshape(-1, t.shape[-1])
        x_mid, tok_l = out_proj_norm([flat(a_l), flat(b_l), flat(c_l), flat(d_l)], w_out_l, x, g1, s2, sh2,
                                     ln1_g[l], ln1_b[l], alpha, N)
        if with_ctx:
            d_c = short_conv_mixer(pc_all[..., 9 * G:12 * G], sc_conv[l])
            xc_mid, tok_c = out_proj_norm([flat(a_c), flat(b_c), flat(c_c), flat(d_c)], w_out_l, xc, g1c, s2c, sh2c,
                                          ln1_g[l], ln1_b[l], alpha, B * L)
            tok = jnp.concatenate([tok_c, tok_l], 0)
        else:
            tok = tok_l
        y_sorted, dest = moe_routed(tok, router_w[l], router_bias[l], exp_w1, exp_w3, exp_w2, l)
        shared_w = (sh_w1[l].astype(BF16), sh_w3[l].astype(BF16), sh_w2[l].astype(BF16))
        if with_ctx:
            xc = moe_finish(y_sorted[dest[:B * L].T], tok_c, xc_mid, *shared_w, g2c, ln2_g[l], ln2_b[l], alpha, B * L)
            dest = dest[B * L:]
        x = moe_finish(y_sorted[dest.T], tok_l, x_mid, *shared_w, g2, ln2_g[l], ln2_b[l], alpha, N)
    return x.reshape(B, N, D)
```

```python
import functools
import math

import jax
import jax.numpy as jnp
from jax import lax
from jax.experimental import pallas as pl
from jax.experimental.pallas import tpu as pltpu

F32 = jnp.float32
BF16 = jnp.bfloat16

GRID_W = 64
GROUP_WIDTH = 512
HEAD_DIM = 128
N_HEADS = GROUP_WIDTH // HEAD_DIM
DN_CHUNK = 64
LRU_C = 8.0
DA_QK = HEAD_DIM // 2
ROPE_BASE = 10000.0
N_EXPERTS = 64
TOP_K = 6
ROUTED_SCALE = 2.5

ATTN_TQ = (1024, 512, 256, 128)
ATTN_TK = (1280, 512, 256, 128)
ATTN_UNROLL = 3

V7X_VMEM_BYTES = 64 * 1024 * 1024
VMEM_LIMIT = V7X_VMEM_BYTES * 3 // 4


def _params(n_axes):
    return pltpu.CompilerParams(dimension_semantics=("arbitrary",) * n_axes, vmem_limit_bytes=VMEM_LIMIT)


def _pick_tile(n, candidates):
    for c in candidates:
        if n % c == 0:
            return c
    return n


def _mod_mm_kernel(a_ref, sc_ref, sh_ref, w_ref, o_ref, a_scr):
    @pl.when(pl.program_id(1) == 0)
    def _():
        a_scr[...] = (a_ref[...] * (1.0 + sc_ref[0]) + sh_ref[0]).astype(BF16)

    o_ref[...] = jnp.dot(a_scr[...], w_ref[...], preferred_element_type=F32)


def modulated_matmul(a, scale, shift, w, rows_per_group):
    M, K = a.shape
    N = w.shape[1]
    tm = _pick_tile(rows_per_group, (1024, 512, 256, 128))
    tn = _pick_tile(N, (512, 256, 128))
    tiles_per_group = rows_per_group // tm
    grp = lambda i, j: (i // tiles_per_group, 0, 0)
    return pl.pallas_call(
        _mod_mm_kernel,
        grid=(M // tm, N // tn),
        in_specs=[pl.BlockSpec((tm, K), lambda i, j: (i, 0)),
                  pl.BlockSpec((1, 1, K), grp),
                  pl.BlockSpec((1, 1, K), grp),
                  pl.BlockSpec((K, tn), lambda i, j: (0, j))],
        out_specs=pl.BlockSpec((tm, tn), lambda i, j: (i, j)),
        out_shape=jax.ShapeDtypeStruct((M, N), F32),
        scratch_shapes=[pltpu.VMEM((tm, K), BF16)],
        compiler_params=_params(2),
        name="modulated_matmul",
    )(a, scale, shift, w)


def _layer_norm_rows(r, g, b, eps):
    mu = jnp.mean(r, axis=-1, keepdims=True)
    d = r - mu
    return d * lax.rsqrt(jnp.mean(d * d, axis=-1, keepdims=True) + eps) * g + b


def _out_proj_kernel(a_ref, b_ref, c_ref, d_ref, w_ref, x_ref, g1_ref, s2_ref, sh2_ref, lg_ref, lb_ref,
                     xmid_ref, tok_ref, *, alpha):
    G = GROUP_WIDTH
    mix = sum(jnp.dot(r[...].astype(BF16), w_ref[g * G:(g + 1) * G, :], preferred_element_type=F32)
              for g, r in enumerate((a_ref, b_ref, c_ref, d_ref)))
    x_mid = _layer_norm_rows(alpha * x_ref[...] + g1_ref[0] * mix, lg_ref[...], lb_ref[...], 1e-5)
    xmid_ref[...] = x_mid
    tok_ref[...] = x_mid * (1.0 + s2_ref[0]) + sh2_ref[0]


def out_proj_norm(mix_parts, w, x, g1, s2, sh2, ln_g, ln_b, alpha, rows_per_group):
    M, D = x.shape
    G = GROUP_WIDTH
    tm = _pick_tile(rows_per_group, (256, 128))
    tiles_per_group = rows_per_group // tm
    grp = lambda i: (i // tiles_per_group, 0, 0)
    row = lambda width: pl.BlockSpec((tm, width), lambda i: (i, 0))
    vec = pl.BlockSpec((1, D), lambda i: (0, 0))
    return pl.pallas_call(
        functools.partial(_out_proj_kernel, alpha=alpha),
        grid=(M // tm,),
        in_specs=[row(G), row(G), row(G), row(G), pl.BlockSpec((D, D), lambda i: (0, 0)), row(D),
                  pl.BlockSpec((1, 1, D), grp), pl.BlockSpec((1, 1, D), grp), pl.BlockSpec((1, 1, D), grp), vec, vec],
        out_specs=[row(D), row(D)],
        out_shape=[jax.ShapeDtypeStruct((M, D), F32), jax.ShapeDtypeStruct((M, D), F32)],
        compiler_params=_params(1),
        name="out_proj_norm",
    )(*mix_parts, w, x, g1, s2, sh2, ln_g.reshape(1, D), ln_b.reshape(1, D))


def _attn_kernel(lam_ref, g_ref, q_ref, kt_ref, v_ref, o_ref, m_scr, acc_scr, *, tk, n_chunks, out_scale):
    q = q_ref[0]
    lane = lax.broadcasted_iota(jnp.int32, q.shape, 1)
    zero = jnp.zeros_like(q)
    q_maps = (jnp.where(lane < DA_QK, q, zero), jnp.where(lane >= DA_QK, q, zero))
    m_scr[...] = jnp.full(m_scr.shape, -jnp.inf, F32)
    acc_scr[...] = jnp.zeros(acc_scr.shape, F32)
    ones = jnp.ones((tk, HEAD_DIM), BF16)

    def chunk(c, carry):
        off = pl.multiple_of(c * tk, tk)
        ktc = kt_ref[0, :, pl.ds(off, tk)]
        v_ext = jnp.concatenate([v_ref[0, pl.ds(off, tk), :], ones], axis=1)
        for mp in range(2):
            s = jnp.dot(q_maps[mp], ktc, preferred_element_type=F32)
            m_prev = m_scr[mp]
            m_new = jnp.maximum(m_prev, jnp.max(s, axis=-1, keepdims=True))
            p = jnp.exp(s - pltpu.repeat(m_new, tk // HEAD_DIM, axis=1))
            alpha = jnp.exp(m_prev - m_new)
            acc_scr[mp] = (pltpu.repeat(alpha, 2, axis=1) * acc_scr[mp]
                           + jnp.dot(p.astype(BF16), v_ext, preferred_element_type=F32))
            m_scr[mp] = m_new
        return carry

    lax.fori_loop(0, n_chunks, chunk, 0, unroll=ATTN_UNROLL)
    a0, a1 = acc_scr[0], acc_scr[1]
    o = a0[:, :HEAD_DIM] / a0[:, HEAD_DIM:] - lam_ref[...] * (a1[:, :HEAD_DIM] / a1[:, HEAD_DIM:])
    o = o * lax.rsqrt(jnp.mean(o * o, axis=-1, keepdims=True) + 1e-6)
    o_ref[0] = o * g_ref[...] * out_scale


def diff_attention(q, kt, v, lam, norm_g, out_scale, n, q_row0, m, k_row0):
    B, _, G = q.shape
    tq = _pick_tile(n, ATTN_TQ)
    tk = _pick_tile(m, ATTN_TK)
    assert q_row0 % tq == 0 and k_row0 % m == 0
    q_blk0, k_blk = q_row0 // tq, k_row0 // m
    kern = functools.partial(_attn_kernel, tk=tk, n_chunks=m // tk, out_scale=out_scale)
    lam_row = jnp.full((1, HEAD_DIM), lam, F32)
    return pl.pallas_call(
        kern,
        grid=(B, N_HEADS, n // tq),
        in_specs=[pl.BlockSpec((1, HEAD_DIM), lambda b, h, i: (0, 0)),
                  pl.BlockSpec((1, HEAD_DIM), lambda b, h, i: (0, 0)),
                  pl.BlockSpec((1, tq, HEAD_DIM), lambda b, h, i: (b, q_blk0 + i, h)),
                  pl.BlockSpec((1, HEAD_DIM, m), lambda b, h, i: (b, h, k_blk)),
                  pl.BlockSpec((1, m, HEAD_DIM), lambda b, h, i: (b, k_blk, h))],
        out_specs=pl.BlockSpec((1, tq, HEAD_DIM), lambda b, h, i: (b, i, h)),
        out_shape=jax.ShapeDtypeStruct((B, n, G), F32),
        scratch_shapes=[pltpu.VMEM((2, tq, HEAD_DIM), F32), pltpu.VMEM((2, tq, 2 * HEAD_DIM), F32)],
        compiler_params=_params(3),
        name="diff_attention",
    )(lam_row, norm_g.reshape(1, HEAD_DIM).astype(F32), q, kt, v)


def _lru_kernel(h0_ref, xprev_ref, x_ref, xnext_ref, cw_ref, cb_ref, wg_ref, bg_ref, sp_ref, *rest,
                tb, n_blocks, reverse, finish):
    if finish:
        hother_ref, gate_ref, out_ref, state_ref, a_scr, b_scr, h_scr = rest
    else:
        out_ref, state_ref, a_scr, b_scr, h_scr = rest
    j = pl.program_id(1)
    t_blk = (n_blocks - 1 - j) if reverse else j
    W = x_ref.shape[-1]

    @pl.when(j == 0)
    def _():
        h_scr[...] = jnp.broadcast_to(h0_ref[0], h_scr.shape)

    prev = jnp.where(t_blk > 0, xprev_ref[0], 0.0)
    nxt = jnp.where(t_blk < n_blocks - 1, xnext_ref[0], 0.0)
    x_ext = jnp.concatenate([prev, x_ref[0], nxt], axis=0)
    cw = cw_ref[...]
    xb = cb_ref[...] + sum(cw[k:k + 1] * x_ext[6 + k:6 + k + tb] for k in range(4))
    xb16 = xb.astype(BF16)
    hw = W // N_HEADS
    pre = [jnp.dot(xb16[:, h * hw:(h + 1) * hw], wg_ref[h], preferred_element_type=F32) for h in range(N_HEADS)]
    pre_r = jnp.concatenate([p[:, :hw] for p in pre], axis=1) + bg_ref[0:1]
    pre_i = jnp.concatenate([p[:, hw:] for p in pre], axis=1) + bg_ref[1:2]
    log_a = -LRU_C * jax.nn.sigmoid(pre_r) * sp_ref[...]
    a = jnp.exp(log_a)
    a_scr[...] = a
    b_scr[...] = jnp.sqrt(-jnp.tanh(log_a) * (a * a + 1.0)) * (jax.nn.sigmoid(pre_i) * xb)

    row = lax.broadcasted_iota(jnp.int32, (8, W), 0)
    n_rows = tb // 8

    def block(i, h):
        blk = (n_rows - 1 - i) if reverse else i
        r0 = pl.multiple_of(blk * 8, 8)
        a = a_scr[pl.ds(r0, 8), :]
        b = b_scr[pl.ds(r0, 8), :]
        for d in (1, 2, 4):
            if reverse:
                keep = row < 8 - d
                a_sh, b_sh = pltpu.roll(a, 8 - d, 0), pltpu.roll(b, 8 - d, 0)
            else:
                keep = row >= d
                a_sh, b_sh = pltpu.roll(a, d, 0), pltpu.roll(b, d, 0)
            b = jnp.where(keep, b + a * b_sh, b)
            a = jnp.where(keep, a * a_sh, a)
        hb = b + a * h
        b_scr[pl.ds(r0, 8), :] = hb
        last = hb[0:1] if reverse else hb[7:8]
        return jnp.broadcast_to(last, (8, W))

    h_fin = lax.fori_loop(0, n_rows, block, h_scr[...], unroll=4)
    h_scr[...] = h_fin
    state_ref[0] = h_fin
    if finish:
        out_ref[0] = (hother_ref[0] + b_scr[...]) * jax.nn.gelu(gate_ref[0])
    else:
        out_ref[0] = b_scr[...]


def lru_scan(p, x_col, gate_col, h0, cw, cb, wg, bg, sp, reverse, h_other=None):
    B, n, _ = p.shape
    W = GROUP_WIDTH
    tb = _pick_tile(n, (512, 256, 128))
    nb = n // tb
    finish = h_other is not None
    tix = (lambda j: nb - 1 - j) if reverse else (lambda j: j)
    r8 = tb // 8
    in_specs = [pl.BlockSpec((1, 8, W), lambda b, j: (b, 0, 0)),
                pl.BlockSpec((1, 8, W), lambda b, j: (b, jnp.maximum(tix(j) * r8 - 1, 0), x_col)),
                pl.BlockSpec((1, tb, W), lambda b, j: (b, tix(j), x_col)),
                pl.BlockSpec((1, 8, W), lambda b, j: (b, jnp.minimum((tix(j) + 1) * r8, n // 8 - 1), x_col)),
                pl.BlockSpec((4, W), lambda b, j: (0, 0)),
                pl.BlockSpec((1, W), lambda b, j: (0, 0)),
                pl.BlockSpec((N_HEADS, W // N_HEADS, 2 * W // N_HEADS), lambda b, j: (0, 0, 0)),
                pl.BlockSpec((2, W), lambda b, j: (0, 0)),
                pl.BlockSpec((1, W), lambda b, j: (0, 0))]
    args = [h0, p, p, p, cw, cb, wg, bg, sp]
    if finish:
        in_specs += [pl.BlockSpec((1, tb, W), lambda b, j: (b, tix(j), 0)),
                     pl.BlockSpec((1, tb, W), lambda b, j: (b, tix(j), gate_col))]
        args += [h_other, p]
    kern = functools.partial(_lru_kernel, tb=tb, n_blocks=nb, reverse=reverse, finish=finish)
    return pl.pallas_call(
        kern,
        grid=(B, nb),
        in_specs=in_specs,
        out_specs=[pl.BlockSpec((1, tb, W), lambda b, j: (b, tix(j), 0)),
                   pl.BlockSpec((1, 8, W), lambda b, j: (b, 0, 0))],
        out_shape=[jax.ShapeDtypeStruct((B, n, W), F32), jax.ShapeDtypeStruct((B, 8, W), F32)],
        scratch_shapes=[pltpu.VMEM((tb, W), F32), pltpu.VMEM((tb, W), F32), pltpu.VMEM((8, W), F32)],
        compiler_params=_params(2),
        name="lru_scan",
    )(*args)


def rglru_mixer(p_ctx, p_lat, x_col, gate_col, conv_w, conv_b, wa, ba, wx, bx, lam):
    B = p_lat.shape[0]
    W = GROUP_WIDTH
    cw = conv_w.T.astype(F32)
    cb = conv_b.reshape(1, W).astype(F32)
    h0 = jnp.zeros((B, 8, W), F32)
    outs = []
    states = {}
    for d, reverse in ((0, False), (1, True)):
        wg = jnp.concatenate([wa[d], wx[d]], axis=-1).astype(BF16)
        bg = jnp.stack([ba[d], bx[d]]).astype(F32)
        sp = jax.nn.softplus(-lam[d]).reshape(1, W).astype(F32)
        states[d] = (wg, bg, sp)
    wg, bg, sp = states[0]
    hc_f, sc_f = lru_scan(p_ctx, x_col, gate_col, h0, cw, cb, wg, bg, sp, False)
    hl_f, _ = lru_scan(p_lat, x_col, gate_col, sc_f, cw, cb, wg, bg, sp, False)
    wg, bg, sp = states[1]
    out_c, sc_b = lru_scan(p_ctx, x_col, gate_col, h0, cw, cb, wg, bg, sp, True, h_other=hc_f)
    out_l, _ = lru_scan(p_lat, x_col, gate_col, sc_b, cw, cb, wg, bg, sp, True, h_other=hl_f)
    return out_c, out_l


def _split_bf16(a):
    hi = a.astype(BF16)
    return hi, (a - hi.astype(F32)).astype(BF16)


def _dot3(a, b):
    ah, al = _split_bf16(a)
    bh, bl = _split_bf16(b)
    d = functools.partial(jnp.dot, preferred_element_type=F32)
    return d(ah, bh) + (d(ah, bl) + d(al, bh))


def _dot_exact_lhs(a_exact, b):
    bh, bl = _split_bf16(b)
    d = functools.partial(jnp.dot, preferred_element_type=F32)
    return d(a_exact, bh) + d(a_exact, bl)


def _gdn_kernel(s0_ref, xprev_ref, x_ref, xnext_ref, gate_ref, cw_ref, ea_ref, dt_ref, ng_ref, *rest,
                tb, n_blocks, reverse, finish, direction):
    if finish:
        oother_ref, z_ref, out_ref, state_ref, s_scr = rest
    else:
        out_ref, state_ref, s_scr = rest
    C = DN_CHUNK
    HD = HEAD_DIM
    j = pl.program_id(1)
    t_blk = (n_blocks - 1 - j) if reverse else j

    @pl.when(j == 0)
    def _():
        s_scr[...] = s0_ref[0]

    prev = jnp.where(t_blk > 0, xprev_ref[0], 0.0)
    nxt = jnp.where(t_blk < n_blocks - 1, xnext_ref[0], 0.0)
    x_ext = jnp.concatenate([prev, x_ref[0], nxt], axis=0)
    cw = cw_ref[...]
    qkv = sum(cw[k:k + 1] * x_ext[6 + k:6 + k + tb] for k in range(4))
    qkv = qkv * jax.nn.sigmoid(qkv)

    gl_in = gate_ref[0]
    g_log = -ea_ref[...] * jax.nn.softplus(gl_in + dt_ref[...])
    beta_all = jax.nn.sigmoid(gl_in)
    r_i = lax.broadcasted_iota(jnp.int32, (tb, tb), 0)
    c_i = lax.broadcasted_iota(jnp.int32, (tb, tb), 1)
    same = (r_i // C) == (c_i // C)
    incl = same & ((c_i >= r_i) if reverse else (c_i <= r_i))
    strict = same & ((c_i > r_i) if reverse else (c_i < r_i))
    ones_where = lambda msk: jnp.where(msk, 1.0, 0.0).astype(BF16)
    g_cum = _dot_exact_lhs(ones_where(incl), g_log)
    g_tot = _dot_exact_lhs(ones_where(same), g_log)
    e_cum = jnp.exp(g_cum)
    e_rest = jnp.exp(g_tot - g_cum)
    e_tot = jnp.exp(g_tot)
    g_cum_t = jnp.transpose(g_cum)
    eye = jnp.where(r_i == c_i, 1.0, 0.0)
    dt = (((1,), (1,)), ((), ()))

    s_cur = [s_scr[h] for h in range(N_HEADS)]
    if finish:
        o_other, z_all = oother_ref[0], z_ref[0]
    heads = range(N_HEADS)
    lane_of = [direction * 8 + h for h in heads]
    col = lambda a, lane: jnp.broadcast_to(a[:, lane:lane + 1], (tb, HD))
    qs, ks, vbs, kbs, attns, ys, ts = [], [], [], [], [], [], []
    for h in heads:
        q = qkv[:, h * HD:(h + 1) * HD]
        k = qkv[:, GROUP_WIDTH + h * HD:GROUP_WIDTH + (h + 1) * HD]
        v = qkv[:, 2 * GROUP_WIDTH + h * HD:2 * GROUP_WIDTH + (h + 1) * HD]
        q = q * lax.rsqrt(jnp.sum(q * q, axis=-1, keepdims=True) + 1e-6) * HD ** -0.5
        k = k * lax.rsqrt(jnp.sum(k * k, axis=-1, keepdims=True) + 1e-6)
        beta = col(beta_all, lane_of[h] + 4)
        kb = k * beta
        k16 = k.astype(BF16)
        decay = jnp.exp(jnp.where(incl, col(g_cum, lane_of[h])[:, :tb] - g_cum_t[lane_of[h]:lane_of[h] + 1, :],
                                  -jnp.inf))
        m_low = jnp.where(strict, lax.dot_general(kb.astype(BF16), k16, dt, preferred_element_type=F32) * decay, 0.0)
        attns.append((lax.dot_general(q.astype(BF16), k16, dt, preferred_element_type=F32) * decay).astype(BF16))
        qs.append(q)
        ks.append(k)
        vbs.append(v * beta)
        kbs.append(kb)
        ys.append(-m_low)
        ts.append(eye - m_low)
    for _ in range(5):
        ys = [_dot3(y, y) for y in ys]
        ts = [t + _dot3(t, y) for t, y in zip(ts, ys)]
    us, ws, k_decs, q_decs = [], [], [], []
    for h in heads:
        rhs = jnp.concatenate([vbs[h], kbs[h] * col(e_cum, lane_of[h])], axis=1).astype(BF16)
        uw = jnp.dot(ts[h].astype(BF16), rhs, preferred_element_type=F32)
        us.append(uw[:, :HD])
        ws.append(uw[:, HD:].astype(BF16))
        k_decs.append((ks[h] * col(e_rest, lane_of[h])).astype(BF16))
        q_decs.append((qs[h] * col(e_cum, lane_of[h])).astype(BF16))
    n_ch = tb // C
    v_new = [[None] * n_ch for _ in heads]
    o_inter = [[None] * n_ch for _ in heads]
    for ci in (range(n_ch - 1, -1, -1) if reverse else range(n_ch)):
        rows = slice(ci * C, (ci + 1) * C)
        for h in heads:
            s16 = s_cur[h].astype(BF16)
            v_new[h][ci] = us[h][rows] - jnp.dot(ws[h][rows], s16, preferred_element_type=F32)
            o_inter[h][ci] = jnp.dot(q_decs[h][rows], s16, preferred_element_type=F32)
        for h in heads:
            s_cur[h] = s_cur[h] * col(e_tot, lane_of[h])[ci * C:ci * C + 1] + lax.dot_general(
                k_decs[h][rows], v_new[h][ci].astype(BF16), (((0,), (0,)), ((), ())), preferred_element_type=F32)
    outs = []
    for h in heads:
        v_all = jnp.concatenate(v_new[h], axis=0).astype(BF16)
        o = jnp.concatenate(o_inter[h], axis=0) + jnp.dot(attns[h], v_all, preferred_element_type=F32)
        if finish:
            o = o + o_other[:, h * HD:(h + 1) * HD]
            o = o * lax.rsqrt(jnp.mean(o * o, axis=-1, keepdims=True) + 1e-6) * ng_ref[...]
            z = z_all[:, h * HD:(h + 1) * HD]
            o = o * (z * jax.nn.sigmoid(z))
        outs.append(o)
    out_ref[0] = jnp.concatenate(outs, axis=1)
    s_fin = jnp.stack(s_cur)
    s_scr[...] = s_fin
    state_ref[0] = s_fin


def gdn_scan(p, pg, s0, cw, ea, dt, ng, direction, o_other=None):
    B, n, _ = p.shape
    G = GROUP_WIDTH
    reverse = direction == 1
    tb = _pick_tile(n, (128,))
    nb = n // tb
    finish = o_other is not None
    tix = (lambda j: nb - 1 - j) if reverse else (lambda j: j)
    r8 = tb // 8
    vec = lambda w: pl.BlockSpec((1, w), lambda b, j: (0, 0))
    in_specs = [pl.BlockSpec((1, N_HEADS, HEAD_DIM, HEAD_DIM), lambda b, j: (b, 0, 0, 0)),
                pl.BlockSpec((1, 8, 3 * G), lambda b, j: (b, jnp.maximum(tix(j) * r8 - 1, 0), 0)),
                pl.BlockSpec((1, tb, 3 * G), lambda b, j: (b, tix(j), 0)),
                pl.BlockSpec((1, 8, 3 * G), lambda b, j: (b, jnp.minimum((tix(j) + 1) * r8, n // 8 - 1), 0)),
                pl.BlockSpec((1, tb, 128), lambda b, j: (b, tix(j), 0)),
                pl.BlockSpec((4, 3 * G), lambda b, j: (0, 0)),
                vec(128), vec(128), vec(HEAD_DIM)]
    args = [s0, p, p, p, pg, cw, ea, dt, ng]
    if finish:
        in_specs += [pl.BlockSpec((1, tb, G), lambda b, j: (b, tix(j), 0)),
                     pl.BlockSpec((1, tb, G), lambda b, j: (b, tix(j), 3))]
        args += [o_other, p]
    kern = functools.partial(_gdn_kernel, tb=tb, n_blocks=nb, reverse=reverse, finish=finish, direction=direction)
    return pl.pallas_call(
        kern,
        grid=(B, nb),
        in_specs=in_specs,
        out_specs=[pl.BlockSpec((1, tb, G), lambda b, j: (b, tix(j), 0)),
                   pl.BlockSpec((1, N_HEADS, HEAD_DIM, HEAD_DIM), lambda b, j: (b, 0, 0, 0))],
        out_shape=[jax.ShapeDtypeStruct((B, n, G), F32),
                   jax.ShapeDtypeStruct((B, N_HEADS, HEAD_DIM, HEAD_DIM), F32)],
        scratch_shapes=[pltpu.VMEM((N_HEADS, HEAD_DIM, HEAD_DIM), F32)],
        compiler_params=_params(2),
        name="gdn_scan",
    )(*args)


def gdn_mixer(p_ctx, g_ctx, p_lat, g_lat, conv_w, a_log, dt_bias, norm_g):
    B = p_lat.shape[0]
    cw = conv_w.T.astype(F32)
    zeros4 = jnp.zeros((2, N_HEADS), F32)
    lanes = lambda t: jnp.pad(jnp.stack([t, zeros4], axis=1).reshape(1, -1), ((0, 0), (0, 128 - 4 * N_HEADS)))
    ea = lanes(jnp.exp(a_log.astype(F32)))
    dt = lanes(dt_bias.astype(F32))
    ng = norm_g.reshape(1, HEAD_DIM).astype(F32)
    s0 = jnp.zeros((B, N_HEADS, HEAD_DIM, HEAD_DIM), F32)
    oc_f, sc_f = gdn_scan(p_ctx, g_ctx, s0, cw, ea, dt, ng, 0)
    ol_f, _ = gdn_scan(p_lat, g_lat, sc_f, cw, ea, dt, ng, 0)
    out_c, sc_b = gdn_scan(p_ctx, g_ctx, s0, cw, ea, dt, ng, 1, o_other=oc_f)
    out_l, _ = gdn_scan(p_lat, g_lat, sc_b, cw, ea, dt, ng, 1, o_other=ol_f)
    return out_c, out_l


def _moe_kernel(be_ref, nused_ref, x_ref, w1_ref, w3_ref, w2_ref, wr_ref, y_ref, w1_scr, w3_scr, w2_scr):
    i = pl.program_id(0)

    @pl.when((i == 0) | (be_ref[i] != be_ref[jnp.maximum(i - 1, 0)]))
    def _():
        w1_scr[...] = w1_ref[0, 0].astype(BF16)
        w3_scr[...] = w3_ref[0, 0].astype(BF16)
        w2_scr[...] = w2_ref[0, 0].astype(BF16)

    @pl.when(i < nused_ref[0])
    def _():
        x = x_ref[...].astype(BF16)
        h1 = jnp.dot(x, w1_scr[...], preferred_element_type=F32)
        h3 = jnp.dot(x, w3_scr[...], preferred_element_type=F32)
        hid = (h1 * jax.nn.sigmoid(h1) * h3).astype(BF16)
        y_ref[...] = jnp.dot(hid, w2_scr[...], preferred_element_type=F32) * wr_ref[...]

    @pl.when(i >= nused_ref[0])
    def _():
        y_ref[...] = jnp.zeros(y_ref.shape, y_ref.dtype)


def grouped_experts(x_sorted, blk_e, n_used, wr, w1, w3, w2, layer, tm):
    R, D = x_sorted.shape
    FF = w1.shape[-1]
    grid_spec = pltpu.PrefetchScalarGridSpec(
        num_scalar_prefetch=2,
        grid=(R // tm,),
        in_specs=[pl.BlockSpec((tm, D), lambda i, be, nu: (i, 0)),
                  pl.BlockSpec((1, 1, D, FF), lambda i, be, nu: (layer, be[i], 0, 0)),
                  pl.BlockSpec((1, 1, D, FF), lambda i, be, nu: (layer, be[i], 0, 0)),
                  pl.BlockSpec((1, 1, FF, D), lambda i, be, nu: (layer, be[i], 0, 0)),
                  pl.BlockSpec((tm, 1), lambda i, be, nu: (i, 0))],
        out_specs=pl.BlockSpec((tm, D), lambda i, be, nu: (i, 0)),
        scratch_shapes=[pltpu.VMEM((D, FF), BF16), pltpu.VMEM((D, FF), BF16), pltpu.VMEM((FF, D), BF16)],
    )
    return pl.pallas_call(
        _moe_kernel,
        grid_spec=grid_spec,
        out_shape=jax.ShapeDtypeStruct((R, D), F32),
        compiler_params=_params(1),
        name="grouped_experts",
    )(blk_e, n_used, x_sorted, w1, w3, w2, wr)


def _moe_finish_kernel(yg_ref, tok_ref, xmid_ref, w1_ref, w3_ref, w2_ref, g2_ref, lg_ref, lb_ref, o_ref, *, alpha):
    x = tok_ref[...].astype(BF16)
    h1 = jnp.dot(x, w1_ref[...], preferred_element_type=F32)
    h3 = jnp.dot(x, w3_ref[...], preferred_element_type=F32)
    hid = (h1 * jax.nn.sigmoid(h1) * h3).astype(BF16)
    y = jnp.sum(yg_ref[...], axis=0) + jnp.dot(hid, w2_ref[...], preferred_element_type=F32)
    o_ref[...] = _layer_norm_rows(alpha * xmid_ref[...] + g2_ref[0] * y, lg_ref[...], lb_ref[...], 1e-5)


def moe_finish(yg, tok, x_mid, sw1, sw3, sw2, g2, ln_g, ln_b, alpha, rows_per_group):
    M, D = tok.shape
    FF = sw1.shape[-1]
    tm = _pick_tile(rows_per_group, (128,))
    tiles_per_group = rows_per_group // tm
    row = pl.BlockSpec((tm, D), lambda i: (i, 0))
    vec = pl.BlockSpec((1, D), lambda i: (0, 0))
    return pl.pallas_call(
        functools.partial(_moe_finish_kernel, alpha=alpha),
        grid=(M // tm,),
        in_specs=[pl.BlockSpec((TOP_K, tm, D), lambda i: (0, i, 0)), row, row,
                  pl.BlockSpec((D, FF), lambda i: (0, 0)),
                  pl.BlockSpec((D, FF), lambda i: (0, 0)),
                  pl.BlockSpec((FF, D), lambda i: (0, 0)),
                  pl.BlockSpec((1, 1, D), lambda i: (i // tiles_per_group, 0, 0)), vec, vec],
        out_specs=row,
        out_shape=jax.ShapeDtypeStruct((M, D), F32),
        compiler_params=_params(1),
        name="moe_finish",
    )(yg, tok, x_mid, sw1, sw3, sw2, g2, ln_g.reshape(1, D), ln_b.reshape(1, D))


def moe_routed(tok, router_w, router_bias, w1, w3, w2, layer):
    T, D = tok.shape
    tm = 512 if T * TOP_K >= 64 * 512 else 128
    scores = jax.nn.sigmoid(jnp.matmul(tok, router_w, preferred_element_type=F32))
    _, idx = lax.top_k(scores + router_bias.astype(F32), TOP_K)
    wts = jnp.take_along_axis(scores, idx, -1)
    wts = wts / jnp.sum(wts, -1, keepdims=True) * ROUTED_SCALE
    A = T * TOP_K
    flat_e = idx.reshape(-1).astype(jnp.int32)
    aid = jnp.arange(A, dtype=jnp.int32)
    _, order, w_sorted = lax.sort((flat_e * A + aid, aid, wts.reshape(-1)), num_keys=1)
    counts = jnp.sum((flat_e[:, None] == jnp.arange(N_EXPERTS, dtype=jnp.int32)[None, :]).astype(jnp.int32), axis=0)
    ends = jnp.cumsum(counts)
    starts = ends - counts
    padded = (counts + tm - 1) // tm * tm
    pad_ends = jnp.cumsum(padded)
    pad_starts = pad_ends - padded
    gap_before = pad_starts - starts
    gap_step = jnp.diff(gap_before, prepend=0)
    shift = jnp.sum(jnp.where(aid[:, None] >= starts[None, :], gap_step[None, :], 0), axis=1)
    _, dest = lax.sort((order, aid + shift), num_keys=1)
    n_blocks = -(-A // tm) + N_EXPERTS
    R = n_blocks * tm
    blk_e = jnp.minimum(jnp.sum((jnp.arange(n_blocks, dtype=jnp.int32)[:, None] * tm >= pad_ends[None, :])
                                .astype(jnp.int32), axis=1), N_EXPERTS - 1)
    n_used = (pad_ends[-1] // tm).astype(jnp.int32).reshape(1)
    row = jnp.arange(R, dtype=jnp.int32).reshape(n_blocks, tm)
    src = row - gap_before[blk_e][:, None]
    valid = (row < (pad_starts + counts)[blk_e][:, None]) & (row < pad_ends[-1])
    src = jnp.where(valid, src, 0).reshape(R)
    buf_tok = jnp.where(valid.reshape(R), order[src] // TOP_K, 0)
    buf_w = jnp.where(valid.reshape(R), w_sorted[src], 0.0)
    y_sorted = grouped_experts(tok[buf_tok], blk_e, n_used, buf_w.reshape(R, 1), w1, w3, w2, layer, tm)
    return y_sorted, dest.reshape(T, TOP_K)


def layer_norm(t, g, b, eps=1e-5):
    mu = jnp.mean(t, -1, keepdims=True)
    var = jnp.mean(jnp.square(t - mu), -1, keepdims=True)
    return (t - mu) * lax.rsqrt(var + eps) * g + b


def rms_norm(t, g, eps=1e-6):
    return t * lax.rsqrt(jnp.mean(jnp.square(t), -1, keepdims=True) + eps) * g


def l2norm(t, eps=1e-6):
    return t * lax.rsqrt(jnp.sum(jnp.square(t), -1, keepdims=True) + eps)


def depthwise_conv(t, w, pad_left, pad_right):
    return lax.conv_general_dilated(t, w.T[:, None, :].astype(t.dtype), window_strides=(1,),
                                    padding=[(pad_left, pad_right)],
                                    dimension_numbers=('NWC', 'WIO', 'NWC'),
                                    feature_group_count=t.shape[-1])


def rope_tables(n):
    rows = n // GRID_W
    n_freq = DA_QK // 4
    inv = ROPE_BASE ** (-jnp.arange(n_freq, dtype=F32) / n_freq)
    row_pos = jnp.repeat(jnp.arange(rows, dtype=F32), GRID_W)
    col_pos = (jnp.arange(n) % GRID_W).astype(F32)
    ang = jnp.concatenate([row_pos[:, None] * inv, col_pos[:, None] * inv], -1)
    cos, sin = jnp.cos(ang), jnp.sin(ang)
    reps = GROUP_WIDTH // DA_QK
    cos_full = jnp.tile(jnp.repeat(cos, 2, axis=1), (1, reps))
    sin_signed = jnp.tile(jnp.stack([-sin, sin], axis=-1).reshape(n, DA_QK), (1, reps))
    return cos_full, sin_signed


def _swap_pairs(x):
    outs = []
    for c in range(x.shape[1] // 128):
        xc = x[:, c * 128:(c + 1) * 128]
        lane = lax.broadcasted_iota(jnp.int32, xc.shape, 1)
        outs.append(jnp.where(lane % 2 == 0, pltpu.roll(xc, 127, 1), pltpu.roll(xc, 1, 1)))
    return jnp.concatenate(outs, axis=1)


def _attn_inputs_kernel(pc_ref, pl_ref, cos_ref, sin_ref, q_ref, kt_ref, v_ref, *, n_lat_blocks, scale):
    G = GROUP_WIDTH
    is_ctx = pl.program_id(1) >= n_lat_blocks
    src = jnp.where(is_ctx, pc_ref[0], pl_ref[0])
    q, k, v = src[:, :G], src[:, G:2 * G], src[:, 2 * G:]
    cos = jnp.where(is_ctx, 1.0, cos_ref[...])
    sin = jnp.where(is_ctx, 0.0, sin_ref[...])
    q_ref[0] = ((q * cos + _swap_pairs(q) * sin) * scale).astype(BF16)
    kt_ref[0] = jnp.transpose(k * cos + _swap_pairs(k) * sin).astype(BF16)
    v_ref[0] = v.astype(BF16)


def attention_inputs(pc_all, pl_all, cos_full, sin_signed, col_blk):
    B, L, _ = pc_all.shape
    n = pl_all.shape[1]
    G = GROUP_WIDTH
    tb = _pick_tile(math.gcd(n, L), (256, 128))
    nlb = n // tb
    kern = functools.partial(_attn_inputs_kernel, n_lat_blocks=nlb, scale=DA_QK ** -0.5)
    lat_blk = lambda j: jnp.minimum(j, nlb - 1)
    return pl.pallas_call(
        kern,
        grid=(B, (n + L) // tb),
        in_specs=[pl.BlockSpec((1, tb, 3 * G), lambda b, j: (b, jnp.maximum(j - nlb, 0), col_blk)),
                  pl.BlockSpec((1, tb, 3 * G), lambda b, j: (b, lat_blk(j), col_blk)),
                  pl.BlockSpec((tb, G), lambda b, j: (lat_blk(j), 0)),
                  pl.BlockSpec((tb, G), lambda b, j: (lat_blk(j), 0))],
        out_specs=[pl.BlockSpec((1, tb, G), lambda b, j: (b, j, 0)),
                   pl.BlockSpec((1, G, tb), lambda b, j: (b, 0, j)),
                   pl.BlockSpec((1, tb, G), lambda b, j: (b, j, 0))],
        out_shape=[jax.ShapeDtypeStruct((B, n + L, G), BF16), jax.ShapeDtypeStruct((B, G, n + L), BF16),
                   jax.ShapeDtypeStruct((B, n + L, G), BF16)],
        compiler_params=_params(2),
        name="attention_inputs",
    )(pc_all, pl_all, cos_full, sin_signed)


def diff_attn_mixer(pc_all, pl_all, cos_full, sin_signed, lam_vecs, norm_g, layer_idx, with_ctx):
    L, n = pc_all.shape[1], pl_all.shape[1]
    lam_init = 0.8 - 0.6 * math.exp(-0.3 * layer_idx)
    lv = lam_vecs.astype(F32)
    lam = jnp.exp(jnp.sum(lv[0] * lv[1])) - jnp.exp(jnp.sum(lv[2] * lv[3])) + lam_init
    q, kt, v = attention_inputs(pc_all, pl_all, cos_full, sin_signed, 2)
    out_l = diff_attention(q, kt, v, lam, norm_g, 1.0 - lam_init, n, 0, n + L, 0)
    out_c = diff_attention(q, kt, v, lam, norm_g, 1.0 - lam_init, L, n, L, n) if with_ctx else None
    return out_c, out_l


def short_conv_mixer(p, conv_w):
    b_g, c_g, xs = jnp.split(p, 3, -1)
    return b_g * depthwise_conv(c_g * xs, conv_w, 1, 1)


def kernel(x, c, ctx, c_ctx, w_ada, b_ada, w_in, dn_conv, dn_a_log, dn_dt_bias, dn_norm, lru_conv, lru_conv_b, lru_wa, lru_ba, lru_wx, lru_bx, lru_lambda, da_lambda, da_norm, sc_conv, w_out, ln1_g, ln1_b, router_w, router_bias, exp_w1, exp_w3, exp_w2, sh_w1, sh_w3, sh_w2, ln2_g, ln2_b):
    depth = w_in.shape[0]
    alpha = (2 * depth) ** 0.25
    B, N, D = x.shape
    L = ctx.shape[1]
    G = GROUP_WIDTH
    silu_c = jax.nn.silu(c)[:, None, :]
    silu_cc = jax.nn.silu(c_ctx)[None, None, :]
    cos_full, sin_signed = rope_tables(N)
    x = x.reshape(B * N, D)
    xc = ctx.reshape(B * L, D)
    n_gate = 4 * N_HEADS
    for l in range(depth):
        with_ctx = l < depth - 1
        sh1, s1, g1, sh2, s2, g2 = jnp.split(silu_c @ w_ada[l] + b_ada[l], 6, -1)
        sh1c, s1c, g1c, sh2c, s2c, g2c = jnp.split(silu_cc @ w_ada[l] + b_ada[l], 6, -1)
        w_l = w_in[l]
        w_main = jnp.concatenate([w_l[:, :4 * G], w_l[:, 4 * G + n_gate:]], axis=1).astype(BF16)
        w_gate = jnp.pad(w_l[:, 4 * G:4 * G + n_gate], ((0, 0), (0, 128 - n_gate))).astype(BF16)
        pl_all = modulated_matmul(x, s1, sh1, w_main, N).reshape(B, N, -1)
        pc_all = modulated_matmul(xc, s1c, sh1c, w_main, B * L).reshape(B, L, -1)
        gl_all = modulated_matmul(x, s1, sh1, w_gate, N).reshape(B, N, -1)
        gc_all = modulated_matmul(xc, s1c, sh1c, w_gate, B * L).reshape(B, L, -1)
        a_c, a_l = gdn_mixer(pc_all, gc_all, pl_all, gl_all, dn_conv[l], dn_a_log[l], dn_dt_bias[l], dn_norm[l])
        b_c, b_l = rglru_mixer(pc_all, pl_all, 4, 5, lru_conv[l], lru_conv_b[l], lru_wa[l], lru_ba[l],
                               lru_wx[l], lru_bx[l], lru_lambda[l])
        c_c, c_l = diff_attn_mixer(pc_all, pl_all, cos_full, sin_signed, da_lambda[l], da_norm[l], l, with_ctx)
        d_l = short_conv_mixer(pl_all[..., 9 * G:12 * G], sc_conv[l])
        w_out_l = w_out[l].astype(BF16)
        flat = lambda t: t.reshape(-1, t.shape[-1])
        x_mid, tok_l = out_proj_norm([flat(a_l), flat(b_l), flat(c_l), flat(d_l)], w_out_l, x, g1, s2, sh2,
                                     ln1_g[l], ln1_b[l], alpha, N)
        if with_ctx:
            d_c = short_conv_mixer(pc_all[..., 9 * G:12 * G], sc_conv[l])
            xc_mid, tok_c = out_proj_norm([flat(a_c), flat(b_c), flat(c_c), flat(d_c)], w_out_l, xc, g1c, s2c, sh2c,
                                          ln1_g[l], ln1_b[l], alpha, B * L)
            tok = jnp.concatenate([tok_c, tok_l], 0)
        else:
            tok = tok_l
        y_sorted, dest = moe_routed(tok, router_w[l], router_bias[l], exp_w1, exp_w3, exp_w2, l)
        shared_w = (sh_w1[l].astype(BF16), sh_w3[l].astype(BF16), sh_w2[l].astype(BF16))
        if with_ctx:
            xc = moe_finish(y_sorted[dest[:B * L].T], tok_c, xc_mid, *shared_w, g2c, ln2_g[l], ln2_b[l], alpha, B * L)
            dest = dest[B * L:]
        x = moe_finish(y_sorted[dest.T], tok_l, x_mid, *shared_w, g2, ln2_g[l], ln2_b[l], alpha, N)
    return x.reshape(B, N, D)
```

```python
import functools
import math

import jax
import jax.numpy as jnp
from jax import lax
from jax.experimental import pallas as pl
from jax.experimental.pallas import tpu as pltpu

F32 = jnp.float32
BF16 = jnp.bfloat16

GRID_W = 64
GROUP_WIDTH = 512
HEAD_DIM = 128
N_HEADS = GROUP_WIDTH // HEAD_DIM
DN_CHUNK = 64
LRU_C = 8.0
DA_QK = HEAD_DIM // 2
ROPE_BASE = 10000.0
N_EXPERTS = 64
TOP_K = 6
ROUTED_SCALE = 2.5

ATTN_TQ = (1024, 512, 256, 128)
ATTN_TK = (1280, 512, 256, 128)
ATTN_UNROLL = 3

V7X_VMEM_BYTES = 64 * 1024 * 1024
VMEM_LIMIT = V7X_VMEM_BYTES * 3 // 4


def _params(n_axes):
    return pltpu.CompilerParams(dimension_semantics=("arbitrary",) * n_axes, vmem_limit_bytes=VMEM_LIMIT)


def _pick_tile(n, candidates):
    for c in candidates:
        if n % c == 0:
            return c
    return n


def _mod_mm_kernel(a_ref, sc_ref, sh_ref, w_ref, o_ref, a_scr):
    @pl.when(pl.program_id(1) == 0)
    def _():
        a_scr[...] = (a_ref[...] * (1.0 + sc_ref[0]) + sh_ref[0]).astype(BF16)

    o_ref[...] = jnp.dot(a_scr[...], w_ref[...], preferred_element_type=F32)


def modulated_matmul(a, scale, shift, w, rows_per_group):
    M, K = a.shape
    N = w.shape[1]
    tm = _pick_tile(rows_per_group, (1024, 512, 256, 128))
    tn = _pick_tile(N, (512, 256, 128))
    tiles_per_group = rows_per_group // tm
    grp = lambda i, j: (i // tiles_per_group, 0, 0)
    return pl.pallas_call(
        _mod_mm_kernel,
        grid=(M // tm, N // tn),
        in_specs=[pl.BlockSpec((tm, K), lambda i, j: (i, 0)),
                  pl.BlockSpec((1, 1, K), grp),
                  pl.BlockSpec((1, 1, K), grp),
                  pl.BlockSpec((K, tn), lambda i, j: (0, j))],
        out_specs=pl.BlockSpec((tm, tn), lambda i, j: (i, j)),
        out_shape=jax.ShapeDtypeStruct((M, N), F32),
        scratch_shapes=[pltpu.VMEM((tm, K), BF16)],
        compiler_params=_params(2),
        name="modulated_matmul",
    )(a, scale, shift, w)


def _layer_norm_rows(r, g, b, eps):
    mu = jnp.mean(r, axis=-1, keepdims=True)
    d = r - mu
    return d * lax.rsqrt(jnp.mean(d * d, axis=-1, keepdims=True) + eps) * g + b


def _out_proj_kernel(a_ref, b_ref, c_ref, d_ref, w_ref, x_ref, g1_ref, s2_ref, sh2_ref, lg_ref, lb_ref,
                     xmid_ref, tok_ref, *, alpha):
    G = GROUP_WIDTH
    mix = sum(jnp.dot(r[...].astype(BF16), w_ref[g * G:(g + 1) * G, :], preferred_element_type=F32)
              for g, r in enumerate((a_ref, b_ref, c_ref, d_ref)))
    x_mid = _layer_norm_rows(alpha * x_ref[...] + g1_ref[0] * mix, lg_ref[...], lb_ref[...], 1e-5)
    xmid_ref[...] = x_mid
    tok_ref[...] = x_mid * (1.0 + s2_ref[0]) + sh2_ref[0]


def out_proj_norm(mix_parts, w, x, g1, s2, sh2, ln_g, ln_b, alpha, rows_per_group):
    M, D = x.shape
    G = GROUP_WIDTH
    tm = _pick_tile(rows_per_group, (256, 128))
    tiles_per_group = rows_per_group // tm
    grp = lambda i: (i // tiles_per_group, 0, 0)
    row = lambda width: pl.BlockSpec((tm, width), lambda i: (i, 0))
    vec = pl.BlockSpec((1, D), lambda i: (0, 0))
    return pl.pallas_call(
        functools.partial(_out_proj_kernel, alpha=alpha),
        grid=(M // tm,),
        in_specs=[row(G), row(G), row(G), row(G), pl.BlockSpec((D, D), lambda i: (0, 0)), row(D),
                  pl.BlockSpec((1, 1, D), grp), pl.BlockSpec((1, 1, D), grp), pl.BlockSpec((1, 1, D), grp), vec, vec],
        out_specs=[row(D), row(D)],
        out_shape=[jax.ShapeDtypeStruct((M, D), F32), jax.ShapeDtypeStruct((M, D), F32)],
        compiler_params=_params(1),
        name="out_proj_norm",
    )(*mix_parts, w, x, g1, s2, sh2, ln_g.reshape(1, D), ln_b.reshape(1, D))


def _attn_kernel(lam_ref, g_ref, q_ref, kt_ref, v_ref, o_ref, m_scr, acc_scr, *, tk, n_chunks, out_scale):
    q = q_ref[0]
    lane = lax.broadcasted_iota(jnp.int32, q.shape, 1)
    zero = jnp.zeros_like(q)
    q_maps = (jnp.where(lane < DA_QK, q, zero), jnp.where(lane >= DA_QK, q, zero))
    m_scr[...] = jnp.full(m_scr.shape, -jnp.inf, F32)
    acc_scr[...] = jnp.zeros(acc_scr.shape, F32)
    ones = jnp.ones((tk, HEAD_DIM), BF16)

    def chunk(c, carry):
        off = pl.multiple_of(c * tk, tk)
        ktc = kt_ref[0, :, pl.ds(off, tk)]
        v_ext = jnp.concatenate([v_ref[0, pl.ds(off, tk), :], ones], axis=1)
        for mp in range(2):
            s = jnp.dot(q_maps[mp], ktc, preferred_element_type=F32)
            m_prev = m_scr[mp]
            m_new = jnp.maximum(m_prev, jnp.max(s, axis=-1, keepdims=True))
            p = jnp.exp(s - pltpu.repeat(m_new, tk // HEAD_DIM, axis=1))
            alpha = jnp.exp(m_prev - m_new)
            acc_scr[mp] = (pltpu.repeat(alpha, 2, axis=1) * acc_scr[mp]
                           + jnp.dot(p.astype(BF16), v_ext, preferred_element_type=F32))
            m_scr[mp] = m_new
        return carry

    lax.fori_loop(0, n_chunks, chunk, 0, unroll=ATTN_UNROLL)
    a0, a1 = acc_scr[0], acc_scr[1]
    o = a0[:, :HEAD_DIM] / a0[:, HEAD_DIM:] - lam_ref[...] * (a1[:, :HEAD_DIM] / a1[:, HEAD_DIM:])
    o = o * lax.rsqrt(jnp.mean(o * o, axis=-1, keepdims=True) + 1e-6)
    o_ref[0] = o * g_ref[...] * out_scale


def diff_attention(q, kt, v, lam, norm_g, out_scale, n, q_row0, m, k_row0):
    B, _, G = q.shape
    tq = _pick_tile(n, ATTN_TQ)
    tk = _pick_tile(m, ATTN_TK)
    assert q_row0 % tq == 0 and k_row0 % m == 0
    q_blk0, k_blk = q_row0 // tq, k_row0 // m
    kern = functools.partial(_attn_kernel, tk=tk, n_chunks=m // tk, out_scale=out_scale)
    lam_row = jnp.full((1, HEAD_DIM), lam, F32)
    return pl.pallas_call(
        kern,
        grid=(B, N_HEADS, n // tq),
        in_specs=[pl.BlockSpec((1, HEAD_DIM), lambda b, h, i: (0, 0)),
                  pl.BlockSpec((1, HEAD_DIM), lambda b, h, i: (0, 0)),
                  pl.BlockSpec((1, tq, HEAD_DIM), lambda b, h, i: (b, q_blk0 + i, h)),
                  pl.BlockSpec((1, HEAD_DIM, m), lambda b, h, i: (b, h, k_blk)),
                  pl.BlockSpec((1, m, HEAD_DIM), lambda b, h, i: (b, k_blk, h))],
        out_specs=pl.BlockSpec((1, tq, HEAD_DIM), lambda b, h, i: (b, i, h)),
        out_shape=jax.ShapeDtypeStruct((B, n, G), F32),
        scratch_shapes=[pltpu.VMEM((2, tq, HEAD_DIM), F32), pltpu.VMEM((2, tq, 2 * HEAD_DIM), F32)],
        compiler_params=_params(3),
        name="diff_attention",
    )(lam_row, norm_g.reshape(1, HEAD_DIM).astype(F32), q, kt, v)


def _lru_kernel(h0_ref, xprev_ref, x_ref, xnext_ref, cw_ref, cb_ref, wg_ref, bg_ref, sp_ref, *rest,
                tb, n_blocks, reverse, finish):
    if finish:
        hother_ref, gate_ref, out_ref, state_ref, a_scr, b_scr, h_scr = rest
    else:
        out_ref, state_ref, a_scr, b_scr, h_scr = rest
    j = pl.program_id(1)
    t_blk = (n_blocks - 1 - j) if reverse else j
    W = x_ref.shape[-1]

    @pl.when(j == 0)
    def _():
        h_scr[...] = jnp.broadcast_to(h0_ref[0], h_scr.shape)

    prev = jnp.where(t_blk > 0, xprev_ref[0], 0.0)
    nxt = jnp.where(t_blk < n_blocks - 1, xnext_ref[0], 0.0)
    x_ext = jnp.concatenate([prev, x_ref[0], nxt], axis=0)
    cw = cw_ref[...]
    xb = cb_ref[...] + sum(cw[k:k + 1] * x_ext[6 + k:6 + k + tb] for k in range(4))
    xb16 = xb.astype(BF16)
    hw = W // N_HEADS
    pre = [jnp.dot(xb16[:, h * hw:(h + 1) * hw], wg_ref[h], preferred_element_type=F32) for h in range(N_HEADS)]
    pre_r = jnp.concatenate([p[:, :hw] for p in pre], axis=1) + bg_ref[0:1]
    pre_i = jnp.concatenate([p[:, hw:] for p in pre], axis=1) + bg_ref[1:2]
    log_a = -LRU_C * jax.nn.sigmoid(pre_r) * sp_ref[...]
    a = jnp.exp(log_a)
    a_scr[...] = a
    b_scr[...] = jnp.sqrt(-jnp.tanh(log_a) * (a * a + 1.0)) * (jax.nn.sigmoid(pre_i) * xb)

    row = lax.broadcasted_iota(jnp.int32, (8, W), 0)
    n_rows = tb // 8

    def block(i, h):
        blk = (n_rows - 1 - i) if reverse else i
        r0 = pl.multiple_of(blk * 8, 8)
        a = a_scr[pl.ds(r0, 8), :]
        b = b_scr[pl.ds(r0, 8), :]
        for d in (1, 2, 4):
            if reverse:
                keep = row < 8 - d
                a_sh, b_sh = pltpu.roll(a, 8 - d, 0), pltpu.roll(b, 8 - d, 0)
            else:
                keep = row >= d
                a_sh, b_sh = pltpu.roll(a, d, 0), pltpu.roll(b, d, 0)
            b = jnp.where(keep, b + a * b_sh, b)
            a = jnp.where(keep, a * a_sh, a)
        hb = b + a * h
        b_scr[pl.ds(r0, 8), :] = hb
        last = hb[0:1] if reverse else hb[7:8]
        return jnp.broadcast_to(last, (8, W))

    h_fin = lax.fori_loop(0, n_rows, block, h_scr[...], unroll=4)
    h_scr[...] = h_fin
    state_ref[0] = h_fin
    if finish:
        out_ref[0] = (hother_ref[0] + b_scr[...]) * jax.nn.gelu(gate_ref[0])
    else:
        out_ref[0] = b_scr[...]


def lru_scan(p, x_col, gate_col, h0, cw, cb, wg, bg, sp, reverse, h_other=None):
    B, n, _ = p.shape
    W = GROUP_WIDTH
    tb = _pick_tile(n, (512, 256, 128))
    nb = n // tb
    finish = h_other is not None
    tix = (lambda j: nb - 1 - j) if reverse else (lambda j: j)
    r8 = tb // 8
    in_specs = [pl.BlockSpec((1, 8, W), lambda b, j: (b, 0, 0)),
                pl.BlockSpec((1, 8, W), lambda b, j: (b, jnp.maximum(tix(j) * r8 - 1, 0), x_col)),
                pl.BlockSpec((1, tb, W), lambda b, j: (b, tix(j), x_col)),
                pl.BlockSpec((1, 8, W), lambda b, j: (b, jnp.minimum((tix(j) + 1) * r8, n // 8 - 1), x_col)),
                pl.BlockSpec((4, W), lambda b, j: (0, 0)),
                pl.BlockSpec((1, W), lambda b, j: (0, 0)),
                pl.BlockSpec((N_HEADS, W // N_HEADS, 2 * W // N_HEADS), lambda b, j: (0, 0, 0)),
                pl.BlockSpec((2, W), lambda b, j: (0, 0)),
                pl.BlockSpec((1, W), lambda b, j: (0, 0))]
    args = [h0, p, p, p, cw, cb, wg, bg, sp]
    if finish:
        in_specs += [pl.BlockSpec((1, tb, W), lambda b, j: (b, tix(j), 0)),
                     pl.BlockSpec((1, tb, W), lambda b, j: (b, tix(j), gate_col))]
        args += [h_other, p]
    kern = functools.partial(_lru_kernel, tb=tb, n_blocks=nb, reverse=reverse, finish=finish)
    return pl.pallas_call(
        kern,
        grid=(B, nb),
        in_specs=in_specs,
        out_specs=[pl.BlockSpec((1, tb, W), lambda b, j: (b, tix(j), 0)),
                   pl.BlockSpec((1, 8, W), lambda b, j: (b, 0, 0))],
        out_shape=[jax.ShapeDtypeStruct((B, n, W), F32), jax.ShapeDtypeStruct((B, 8, W), F32)],
        scratch_shapes=[pltpu.VMEM((tb, W), F32), pltpu.VMEM((tb, W), F32), pltpu.VMEM((8, W), F32)],
        compiler_params=_params(2),
        name="lru_scan",
    )(*args)


def rglru_mixer(p_ctx, p_lat, x_col, gate_col, conv_w, conv_b, wa, ba, wx, bx, lam):
    B = p_lat.shape[0]
    W = GROUP_WIDTH
    cw = conv_w.T.astype(F32)
    cb = conv_b.reshape(1, W).astype(F32)
    h0 = jnp.zeros((B, 8, W), F32)
    outs = []
    states = {}
    for d, reverse in ((0, False), (1, True)):
        wg = jnp.concatenate([wa[d], wx[d]], axis=-1).astype(BF16)
        bg = jnp.stack([ba[d], bx[d]]).astype(F32)
        sp = jax.nn.softplus(-lam[d]).reshape(1, W).astype(F32)
        states[d] = (wg, bg, sp)
    wg, bg, sp = states[0]
    hc_f, sc_f = lru_scan(p_ctx, x_col, gate_col, h0, cw, cb, wg, bg, sp, False)
    hl_f, _ = lru_scan(p_lat, x_col, gate_col, sc_f, cw, cb, wg, bg, sp, False)
    wg, bg, sp = states[1]
    out_c, sc_b = lru_scan(p_ctx, x_col, gate_col, h0, cw, cb, wg, bg, sp, True, h_other=hc_f)
    out_l, _ = lru_scan(p_lat, x_col, gate_col, sc_b, cw, cb, wg, bg, sp, True, h_other=hl_f)
    return out_c, out_l


def _split_bf16(a):
    hi = a.astype(BF16)
    return hi, (a - hi.astype(F32)).astype(BF16)


def _dot3(a, b):
    ah, al = _split_bf16(a)
    bh, bl = _split_bf16(b)
    d = functools.partial(jnp.dot, preferred_element_type=F32)
    return d(ah, bh) + (d(ah, bl) + d(al, bh))


def _dot_exact_lhs(a_exact, b):
    bh, bl = _split_bf16(b)
    d = functools.partial(jnp.dot, preferred_element_type=F32)
    return d(a_exact, bh) + d(a_exact, bl)


def _gdn_kernel(s0_ref, xprev_ref, x_ref, xnext_ref, gate_ref, cw_ref, ea_ref, dt_ref, ng_ref, *rest,
                tb, n_blocks, reverse, finish, direction):
    if finish:
        oother_ref, z_ref, out_ref, state_ref, s_scr = rest
    else:
        out_ref, state_ref, s_scr = rest
    C = DN_CHUNK
    HD = HEAD_DIM
    j = pl.program_id(1)
    t_blk = (n_blocks - 1 - j) if reverse else j

    @pl.when(j == 0)
    def _():
        s_scr[...] = s0_ref[0]

    prev = jnp.where(t_blk > 0, xprev_ref[0], 0.0)
    nxt = jnp.where(t_blk < n_blocks - 1, xnext_ref[0], 0.0)
    x_ext = jnp.concatenate([prev, x_ref[0], nxt], axis=0)
    cw = cw_ref[...]
    qkv = sum(cw[k:k + 1] * x_ext[6 + k:6 + k + tb] for k in range(4))
    qkv = qkv * jax.nn.sigmoid(qkv)

    gl_in = gate_ref[0]
    g_log = -ea_ref[...] * jax.nn.softplus(gl_in + dt_ref[...])
    beta_all = jax.nn.sigmoid(gl_in)
    r_i = lax.broadcasted_iota(jnp.int32, (tb, tb), 0)
    c_i = lax.broadcasted_iota(jnp.int32, (tb, tb), 1)
    same = (r_i // C) == (c_i // C)
    incl = same & ((c_i >= r_i) if reverse else (c_i <= r_i))
    strict = same & ((c_i > r_i) if reverse else (c_i < r_i))
    ones_where = lambda msk: jnp.where(msk, 1.0, 0.0).astype(BF16)
    g_cum = _dot_exact_lhs(ones_where(incl), g_log)
    g_tot = _dot_exact_lhs(ones_where(same), g_log)
    e_cum = jnp.exp(g_cum)
    e_rest = jnp.exp(g_tot - g_cum)
    e_tot = jnp.exp(g_tot)
    g_cum_t = jnp.transpose(g_cum)
    eye = jnp.where(r_i == c_i, 1.0, 0.0)
    dt = (((1,), (1,)), ((), ()))

    s_cur = [s_scr[h] for h in range(N_HEADS)]
    if finish:
        o_other, z_all = oother_ref[0], z_ref[0]
    heads = range(N_HEADS)
    lane_of = [direction * 8 + h for h in heads]
    col = lambda a, lane: jnp.broadcast_to(a[:, lane:lane + 1], (tb, HD))
    qs, ks, vbs, kbs, attns, ys, ts = [], [], [], [], [], [], []
    for h in heads:
        q = qkv[:, h * HD:(h + 1) * HD]
        k = qkv[:, GROUP_WIDTH + h * HD:GROUP_WIDTH + (h + 1) * HD]
        v = qkv[:, 2 * GROUP_WIDTH + h * HD:2 * GROUP_WIDTH + (h + 1) * HD]
        q = q * lax.rsqrt(jnp.sum(q * q, axis=-1, keepdims=True) + 1e-6) * HD ** -0.5
        k = k * lax.rsqrt(jnp.sum(k * k, axis=-1, keepdims=True) + 1e-6)
        beta = col(beta_all, lane_of[h] + 4)
        kb = k * beta
        k16 = k.astype(BF16)
        decay = jnp.exp(jnp.where(incl, col(g_cum, lane_of[h])[:, :tb] - g_cum_t[lane_of[h]:lane_of[h] + 1, :],
                                  -jnp.inf))
        m_low = jnp.where(strict, lax.dot_general(kb.astype(BF16), k16, dt, preferred_element_type=F32) * decay, 0.0)
        attns.append((lax.dot_general(q.astype(BF16), k16, dt, preferred_element_type=F32) * decay).astype(BF16))
        qs.append(q)
        ks.append(k)
        vbs.append(v * beta)
        kbs.append(kb)
        ys.append(-m_low)
        ts.append(eye - m_low)
    for _ in range(5):
        ys = [_dot3(y, y) for y in ys]
        ts = [t + _dot3(t, y) for t, y in zip(ts, ys)]
    us, ws, k_decs, q_decs = [], [], [], []
    for h in heads:
        rhs = jnp.concatenate([vbs[h], kbs[h] * col(e_cum, lane_of[h])], axis=1).astype(BF16)
        uw = jnp.dot(ts[h].astype(BF16), rhs, preferred_element_type=F32)
        us.append(uw[:, :HD])
        ws.append(uw[:, HD:].astype(BF16))
        k_decs.append((ks[h] * col(e_rest, lane_of[h])).astype(BF16))
        q_decs.append((qs[h] * col(e_cum, lane_of[h])).astype(BF16))
    n_ch = tb // C
    v_new = [[None] * n_ch for _ in heads]
    o_inter = [[None] * n_ch for _ in heads]
    for ci in (range(n_ch - 1, -1, -1) if reverse else range(n_ch)):
        rows = slice(ci * C, (ci + 1) * C)
        for h in heads:
            s16 = s_cur[h].astype(BF16)
            v_new[h][ci] = us[h][rows] - jnp.dot(ws[h][rows], s16, preferred_element_type=F32)
            o_inter[h][ci] = jnp.dot(q_decs[h][rows], s16, preferred_element_type=F32)
        for h in heads:
            s_cur[h] = s_cur[h] * col(e_tot, lane_of[h])[ci * C:ci * C + 1] + lax.dot_general(
                k_decs[h][rows], v_new[h][ci].astype(BF16), (((0,), (0,)), ((), ())), preferred_element_type=F32)
    outs = []
    for h in heads:
        v_all = jnp.concatenate(v_new[h], axis=0).astype(BF16)
        o = jnp.concatenate(o_inter[h], axis=0) + jnp.dot(attns[h], v_all, preferred_element_type=F32)
        if finish:
            o = o + o_other[:, h * HD:(h + 1) * HD]
            o = o * lax.rsqrt(jnp.mean(o * o, axis=-1, keepdims=True) + 1e-6) * ng_ref[...]
            z = z_all[:, h * HD:(h + 1) * HD]
            o = o * (z * jax.nn.sigmoid(z))
        outs.append(o)
    out_ref[0] = jnp.concatenate(outs, axis=1)
    s_fin = jnp.stack(s_cur)
    s_scr[...] = s_fin
    state_ref[0] = s_fin


def gdn_scan(p, pg, s0, cw, ea, dt, ng, direction, o_other=None):
    B, n, _ = p.shape
    G = GROUP_WIDTH
    reverse = direction == 1
    tb = _pick_tile(n, (128,))
    nb = n // tb
    finish = o_other is not None
    tix = (lambda j: nb - 1 - j) if reverse else (lambda j: j)
    r8 = tb // 8
    vec = lambda w: pl.BlockSpec((1, w), lambda b, j: (0, 0))
    in_specs = [pl.BlockSpec((1, N_HEADS, HEAD_DIM, HEAD_DIM), lambda b, j: (b, 0, 0, 0)),
                pl.BlockSpec((1, 8, 3 * G), lambda b, j: (b, jnp.maximum(tix(j) * r8 - 1, 0), 0)),
                pl.BlockSpec((1, tb, 3 * G), lambda b, j: (b, tix(j), 0)),
                pl.BlockSpec((1, 8, 3 * G), lambda b, j: (b, jnp.minimum((tix(j) + 1) * r8, n // 8 - 1), 0)),
                pl.BlockSpec((1, tb, 128), lambda b, j: (b, tix(j), 0)),
                pl.BlockSpec((4, 3 * G), lambda b, j: (0, 0)),
                vec(128), vec(128), vec(HEAD_DIM)]
    args = [s0, p, p, p, pg, cw, ea, dt, ng]
    if finish:
        in_specs += [pl.BlockSpec((1, tb, G), lambda b, j: (b, tix(j), 0)),
                     pl.BlockSpec((1, tb, G), lambda b, j: (b, tix(j), 3))]
        args += [o_other, p]
    kern = functools.partial(_gdn_kernel, tb=tb, n_blocks=nb, reverse=reverse, finish=finish, direction=direction)
    return pl.pallas_call(
        kern,
        grid=(B, nb),
        in_specs=in_specs,
        out_specs=[pl.BlockSpec((1, tb, G), lambda b, j: (b, tix(j), 0)),
                   pl.BlockSpec((1, N_HEADS, HEAD_DIM, HEAD_DIM), lambda b, j: (b, 0, 0, 0))],
        out_shape=[jax.ShapeDtypeStruct((B, n, G), F32),
                   jax.ShapeDtypeStruct((B, N_HEADS, HEAD_DIM, HEAD_DIM), F32)],
        scratch_shapes=[pltpu.VMEM((N_HEADS, HEAD_DIM, HEAD_DIM), F32)],
        compiler_params=_params(2),
        name="gdn_scan",
    )(*args)


def gdn_mixer(p_ctx, g_ctx, p_lat, g_lat, conv_w, a_log, dt_bias, norm_g):
    B = p_lat.shape[0]
    cw = conv_w.T.astype(F32)
    zeros4 = jnp.zeros((2, N_HEADS), F32)
    lanes = lambda t: jnp.pad(jnp.stack([t, zeros4], axis=1).reshape(1, -1), ((0, 0), (0, 128 - 4 * N_HEADS)))
    ea = lanes(jnp.exp(a_log.astype(F32)))
    dt = lanes(dt_bias.astype(F32))
    ng = norm_g.reshape(1, HEAD_DIM).astype(F32)
    s0 = jnp.zeros((B, N_HEADS, HEAD_DIM, HEAD_DIM), F32)
    oc_f, sc_f = gdn_scan(p_ctx, g_ctx, s0, cw, ea, dt, ng, 0)
    ol_f, _ = gdn_scan(p_lat, g_lat, sc_f, cw, ea, dt, ng, 0)
    out_c, sc_b = gdn_scan(p_ctx, g_ctx, s0, cw, ea, dt, ng, 1, o_other=oc_f)
    out_l, _ = gdn_scan(p_lat, g_lat, sc_b, cw, ea, dt, ng, 1, o_other=ol_f)
    return out_c, out_l


def _moe_kernel(be_ref, nused_ref, x_ref, w1_ref, w3_ref, w2_ref, wr_ref, y_ref, w1_scr, w3_scr, w2_scr):
    i = pl.program_id(0)

    @pl.when((i == 0) | (be_ref[i] != be_ref[jnp.maximum(i - 1, 0)]))
    def _():
        w1_scr[...] = w1_ref[0, 0].astype(BF16)
        w3_scr[...] = w3_ref[0, 0].astype(BF16)
        w2_scr[...] = w2_ref[0, 0].astype(BF16)

    @pl.when(i < nused_ref[0])
    def _():
        x = x_ref[...].astype(BF16)
        h1 = jnp.dot(x, w1_scr[...], preferred_element_type=F32)
        h3 = jnp.dot(x, w3_scr[...], preferred_element_type=F32)
        hid = (h1 * jax.nn.sigmoid(h1) * h3).astype(BF16)
        y_ref[...] = jnp.dot(hid, w2_scr[...], preferred_element_type=F32) * wr_ref[...]

    @pl.when(i >= nused_ref[0])
    def _():
        y_ref[...] = jnp.zeros(y_ref.shape, y_ref.dtype)


def grouped_experts(x_sorted, blk_e, n_used, wr, w1, w3, w2, layer, tm):
    R, D = x_sorted.shape
    FF = w1.shape[-1]
    grid_spec = pltpu.PrefetchScalarGridSpec(
        num_scalar_prefetch=2,
        grid=(R // tm,),
        in_specs=[pl.BlockSpec((tm, D), lambda i, be, nu: (i, 0)),
                  pl.BlockSpec((1, 1, D, FF), lambda i, be, nu: (layer, be[i], 0, 0)),
                  pl.BlockSpec((1, 1, D, FF), lambda i, be, nu: (layer, be[i], 0, 0)),
                  pl.BlockSpec((1, 1, FF, D), lambda i, be, nu: (layer, be[i], 0, 0)),
                  pl.BlockSpec((tm, 1), lambda i, be, nu: (i, 0))],
        out_specs=pl.BlockSpec((tm, D), lambda i, be, nu: (i, 0)),
        scratch_shapes=[pltpu.VMEM((D, FF), BF16), pltpu.VMEM((D, FF), BF16), pltpu.VMEM((FF, D), BF16)],
    )
    return pl.pallas_call(
        _moe_kernel,
        grid_spec=grid_spec,
        out_shape=jax.ShapeDtypeStruct((R, D), F32),
        compiler_params=_params(1),
        name="grouped_experts",
    )(blk_e, n_used, x_sorted, w1, w3, w2, wr)


def _moe_finish_kernel(yg_ref, tok_ref, xmid_ref, w1_ref, w3_ref, w2_ref, g2_ref, lg_ref, lb_ref, o_ref, *, alpha):
    x = tok_ref[...].astype(BF16)
    h1 = jnp.dot(x, w1_ref[...], preferred_element_type=F32)
    h3 = jnp.dot(x, w3_ref[...], preferred_element_type=F32)
    hid = (h1 * jax.nn.sigmoid(h1) * h3).astype(BF16)
    y = jnp.sum(yg_ref[...], axis=0) + jnp.dot(hid, w2_ref[...], preferred_element_type=F32)
    o_ref[...] = _layer_norm_rows(alpha * xmid_ref[...] + g2_ref[0] * y, lg_ref[...], lb_ref[...], 1e-5)


def moe_finish(yg, tok, x_mid, sw1, sw3, sw2, g2, ln_g, ln_b, alpha, rows_per_group):
    M, D = tok.shape
    FF = sw1.shape[-1]
    tm = _pick_tile(rows_per_group, (128,))
    tiles_per_group = rows_per_group // tm
    row = pl.BlockSpec((tm, D), lambda i: (i, 0))
    vec = pl.BlockSpec((1, D), lambda i: (0, 0))
    return pl.pallas_call(
        functools.partial(_moe_finish_kernel, alpha=alpha),
        grid=(M // tm,),
        in_specs=[pl.BlockSpec((TOP_K, tm, D), lambda i: (0, i, 0)), row, row,
                  pl.BlockSpec((D, FF), lambda i: (0, 0)),
                  pl.BlockSpec((D, FF), lambda i: (0, 0)),
                  pl.BlockSpec((FF, D), lambda i: (0, 0)),
                  pl.BlockSpec((1, 1, D), lambda i: (i // tiles_per_group, 0, 0)), vec, vec],
        out_specs=row,
        out_shape=jax.ShapeDtypeStruct((M, D), F32),
        compiler_params=_params(1),
        name="moe_finish",
    )(yg, tok, x_mid, sw1, sw3, sw2, g2, ln_g.reshape(1, D), ln_b.reshape(1, D))


def moe_routed(tok, router_w, router_bias, w1, w3, w2, layer):
    T, D = tok.shape
    tm = 512 if T * TOP_K >= 64 * 512 else 128
    scores = jax.nn.sigmoid(jnp.matmul(tok, router_w, preferred_element_type=F32))
    _, idx = lax.top_k(scores + router_bias.astype(F32), TOP_K)
    wts = jnp.take_along_axis(scores, idx, -1)
    wts = wts / jnp.sum(wts, -1, keepdims=True) * ROUTED_SCALE
    A = T * TOP_K
    flat_e = idx.reshape(-1).astype(jnp.int32)
    aid = jnp.arange(A, dtype=jnp.int32)
    _, order, w_sorted = lax.sort((flat_e * A + aid, aid, wts.reshape(-1)), num_keys=1)
    counts = jnp.sum((flat_e[:, None] == jnp.arange(N_EXPERTS, dtype=jnp.int32)[None, :]).astype(jnp.int32), axis=0)
    ends = jnp.cumsum(counts)
    starts = ends - counts
    padded = (counts + tm - 1) // tm * tm
    pad_ends = jnp.cumsum(padded)
    pad_starts = pad_ends - padded
    gap_before = pad_starts - starts
    gap_step = jnp.diff(gap_before, prepend=0)
    shift = jnp.sum(jnp.where(aid[:, None] >= starts[None, :], gap_step[None, :], 0), axis=1)
    _, dest = lax.sort((order, aid + shift), num_keys=1)
    n_blocks = -(-A // tm) + N_EXPERTS
    R = n_blocks * tm
    blk_e = jnp.minimum(jnp.sum((jnp.arange(n_blocks, dtype=jnp.int32)[:, None] * tm >= pad_ends[None, :])
                                .astype(jnp.int32), axis=1), N_EXPERTS - 1)
    n_used = (pad_ends[-1] // tm).astype(jnp.int32).reshape(1)
    row = jnp.arange(R, dtype=jnp.int32).reshape(n_blocks, tm)
    src = row - gap_before[blk_e][:, None]
    valid = (row < (pad_starts + counts)[blk_e][:, None]) & (row < pad_ends[-1])
    src = jnp.where(valid, src, 0).reshape(R)
    buf_tok = jnp.where(valid.reshape(R), order[src] // TOP_K, 0)
    buf_w = jnp.where(valid.reshape(R), w_sorted[src], 0.0)
    y_sorted = grouped_experts(tok[buf_tok], blk_e, n_used, buf_w.reshape(R, 1), w1, w3, w2, layer, tm)
    return y_sorted, dest.reshape(T, TOP_K)


def _sconv_kernel(b_ref, cprev_ref, c_ref, cnext_ref, xprev_ref, x_ref, xnext_ref, cw_ref, o_ref, *, tb, n_blocks):
    j = pl.program_id(1)
    prev = jnp.where(j > 0, cprev_ref[0] * xprev_ref[0], 0.0)
    nxt = jnp.where(j < n_blocks - 1, cnext_ref[0] * xnext_ref[0], 0.0)
    u_ext = jnp.concatenate([prev, c_ref[0] * x_ref[0], nxt], axis=0)
    cw = cw_ref[...]
    o_ref[0] = b_ref[0] * sum(cw[k:k + 1] * u_ext[7 + k:7 + k + tb] for k in range(3))


def short_conv_mixer(p, b_col, c_col, x_col, conv_w):
    B, n, _ = p.shape
    W = GROUP_WIDTH
    tb = _pick_tile(n, (512, 256, 128))
    nb = n // tb
    r8 = tb // 8
    cur = lambda col: pl.BlockSpec((1, tb, W), lambda b, j: (b, j, col))
    before = lambda col: pl.BlockSpec((1, 8, W), lambda b, j: (b, jnp.maximum(j * r8 - 1, 0), col))
    after = lambda col: pl.BlockSpec((1, 8, W), lambda b, j: (b, jnp.minimum((j + 1) * r8, n // 8 - 1), col))
    return pl.pallas_call(
        functools.partial(_sconv_kernel, tb=tb, n_blocks=nb),
        grid=(B, nb),
        in_specs=[cur(b_col), before(c_col), cur(c_col), after(c_col), before(x_col), cur(x_col), after(x_col),
                  pl.BlockSpec((3, W), lambda b, j: (0, 0))],
        out_specs=pl.BlockSpec((1, tb, W), lambda b, j: (b, j, 0)),
        out_shape=jax.ShapeDtypeStruct((B, n, W), F32),
        compiler_params=_params(2),
        name="short_conv",
    )(p, p, p, p, p, p, p, conv_w.T.astype(F32))


def rope_tables(n):
    rows = n // GRID_W
    n_freq = DA_QK // 4
    inv = ROPE_BASE ** (-jnp.arange(n_freq, dtype=F32) / n_freq)
    row_pos = jnp.repeat(jnp.arange(rows, dtype=F32), GRID_W)
    col_pos = (jnp.arange(n) % GRID_W).astype(F32)
    ang = jnp.concatenate([row_pos[:, None] * inv, col_pos[:, None] * inv], -1)
    cos, sin = jnp.cos(ang), jnp.sin(ang)
    reps = GROUP_WIDTH // DA_QK
    cos_full = jnp.tile(jnp.repeat(cos, 2, axis=1), (1, reps))
    sin_signed = jnp.tile(jnp.stack([-sin, sin], axis=-1).reshape(n, DA_QK), (1, reps))
    return cos_full, sin_signed


def _swap_pairs(x):
    outs = []
    for c in range(x.shape[1] // 128):
        xc = x[:, c * 128:(c + 1) * 128]
        lane = lax.broadcasted_iota(jnp.int32, xc.shape, 1)
        outs.append(jnp.where(lane % 2 == 0, pltpu.roll(xc, 127, 1), pltpu.roll(xc, 1, 1)))
    return jnp.concatenate(outs, axis=1)


def _attn_inputs_kernel(pc_ref, pl_ref, cos_ref, sin_ref, q_ref, kt_ref, v_ref, *, n_lat_blocks, scale):
    G = GROUP_WIDTH
    is_ctx = pl.program_id(1) >= n_lat_blocks
    src = jnp.where(is_ctx, pc_ref[0], pl_ref[0])
    q, k, v = src[:, :G], src[:, G:2 * G], src[:, 2 * G:]
    cos = jnp.where(is_ctx, 1.0, cos_ref[...])
    sin = jnp.where(is_ctx, 0.0, sin_ref[...])
    q_ref[0] = ((q * cos + _swap_pairs(q) * sin) * scale).astype(BF16)
    kt_ref[0] = jnp.transpose(k * cos + _swap_pairs(k) * sin).astype(BF16)
    v_ref[0] = v.astype(BF16)


def attention_inputs(pc_all, pl_all, cos_full, sin_signed, col_blk):
    B, L, _ = pc_all.shape
    n = pl_all.shape[1]
    G = GROUP_WIDTH
    tb = _pick_tile(math.gcd(n, L), (256, 128))
    nlb = n // tb
    kern = functools.partial(_attn_inputs_kernel, n_lat_blocks=nlb, scale=DA_QK ** -0.5)
    lat_blk = lambda j: jnp.minimum(j, nlb - 1)
    return pl.pallas_call(
        kern,
        grid=(B, (n + L) // tb),
        in_specs=[pl.BlockSpec((1, tb, 3 * G), lambda b, j: (b, jnp.maximum(j - nlb, 0), col_blk)),
                  pl.BlockSpec((1, tb, 3 * G), lambda b, j: (b, lat_blk(j), col_blk)),
                  pl.BlockSpec((tb, G), lambda b, j: (lat_blk(j), 0)),
                  pl.BlockSpec((tb, G), lambda b, j: (lat_blk(j), 0))],
        out_specs=[pl.BlockSpec((1, tb, G), lambda b, j: (b, j, 0)),
                   pl.BlockSpec((1, G, tb), lambda b, j: (b, 0, j)),
                   pl.BlockSpec((1, tb, G), lambda b, j: (b, j, 0))],
        out_shape=[jax.ShapeDtypeStruct((B, n + L, G), BF16), jax.ShapeDtypeStruct((B, G, n + L), BF16),
                   jax.ShapeDtypeStruct((B, n + L, G), BF16)],
        compiler_params=_params(2),
        name="attention_inputs",
    )(pc_all, pl_all, cos_full, sin_signed)


def diff_attn_mixer(pc_all, pl_all, cos_full, sin_signed, lam_vecs, norm_g, layer_idx, with_ctx):
    L, n = pc_all.shape[1], pl_all.shape[1]
    lam_init = 0.8 - 0.6 * math.exp(-0.3 * layer_idx)
    lv = lam_vecs.astype(F32)
    lam = jnp.exp(jnp.sum(lv[0] * lv[1])) - jnp.exp(jnp.sum(lv[2] * lv[3])) + lam_init
    q, kt, v = attention_inputs(pc_all, pl_all, cos_full, sin_signed, 2)
    out_l = diff_attention(q, kt, v, lam, norm_g, 1.0 - lam_init, n, 0, n + L, 0)
    out_c = diff_attention(q, kt, v, lam, norm_g, 1.0 - lam_init, L, n, L, n) if with_ctx else None
    return out_c, out_l


def kernel(x, c, ctx, c_ctx, w_ada, b_ada, w_in, dn_conv, dn_a_log, dn_dt_bias, dn_norm, lru_conv, lru_conv_b, lru_wa, lru_ba, lru_wx, lru_bx, lru_lambda, da_lambda, da_norm, sc_conv, w_out, ln1_g, ln1_b, router_w, router_bias, exp_w1, exp_w3, exp_w2, sh_w1, sh_w3, sh_w2, ln2_g, ln2_b):
    depth = w_in.shape[0]
    alpha = (2 * depth) ** 0.25
    B, N, D = x.shape
    L = ctx.shape[1]
    G = GROUP_WIDTH
    silu_c = jax.nn.silu(c)[:, None, :]
    silu_cc = jax.nn.silu(c_ctx)[None, None, :]
    cos_full, sin_signed = rope_tables(N)
    x = x.reshape(B * N, D)
    xc = ctx.reshape(B * L, D)
    n_gate = 4 * N_HEADS
    for l in range(depth):
        with_ctx = l < depth - 1
        sh1, s1, g1, sh2, s2, g2 = jnp.split(silu_c @ w_ada[l] + b_ada[l], 6, -1)
        sh1c, s1c, g1c, sh2c, s2c, g2c = jnp.split(silu_cc @ w_ada[l] + b_ada[l], 6, -1)
        w_l = w_in[l]
        w_main = jnp.concatenate([w_l[:, :4 * G], w_l[:, 4 * G + n_gate:]], axis=1).astype(BF16)
        w_gate = jnp.pad(w_l[:, 4 * G:4 * G + n_gate], ((0, 0), (0, 128 - n_gate))).astype(BF16)
        pl_all = modulated_matmul(x, s1, sh1, w_main, N).reshape(B, N, -1)
        pc_all = modulated_matmul(xc, s1c, sh1c, w_main, B * L).reshape(B, L, -1)
        gl_all = modulated_matmul(x, s1, sh1, w_gate, N).reshape(B, N, -1)
        gc_all = modulated_matmul(xc, s1c, sh1c, w_gate, B * L).reshape(B, L, -1)
        a_c, a_l = gdn_mixer(pc_all, gc_all, pl_all, gl_all, dn_conv[l], dn_a_log[l], dn_dt_bias[l], dn_norm[l])
        b_c, b_l = rglru_mixer(pc_all, pl_all, 4, 5, lru_conv[l], lru_conv_b[l], lru_wa[l], lru_ba[l],
                               lru_wx[l], lru_bx[l], lru_lambda[l])
        c_c, c_l = diff_attn_mixer(pc_all, pl_all, cos_full, sin_signed, da_lambda[l], da_norm[l], l, with_ctx)
        d_l = short_conv_mixer(pl_all, 9, 10, 11, sc_conv[l])
        w_out_l = w_out[l].astype(BF16)
        flat = lambda t: t.reshape(-1, t.shape[-1])
        x_mid, tok_l = out_proj_norm([flat(a_l), flat(b_l), flat(c_l), flat(d_l)], w_out_l, x, g1, s2, sh2,
                                     ln1_g[l], ln1_b[l], alpha, N)
        if with_ctx:
            d_c = short_conv_mixer(pc_all, 9, 10, 11, sc_conv[l])
            xc_mid, tok_c = out_proj_norm([flat(a_c), flat(b_c), flat(c_c), flat(d_c)], w_out_l, xc, g1c, s2c, sh2c,
                                          ln1_g[l], ln1_b[l], alpha, B * L)
            tok = jnp.concatenate([tok_c, tok_l], 0)
        else:
            tok = tok_l
        y_sorted, dest = moe_routed(tok, router_w[l], router_bias[l], exp_w1, exp_w3, exp_w2, l)
        shared_w = (sh_w1[l].astype(BF16), sh_w3[l].astype(BF16), sh_w2[l].astype(BF16))
        if with_ctx:
            xc = moe_finish(y_sorted[dest[:B * L].T], tok_c, xc_mid, *shared_w, g2c, ln2_g[l], ln2_b[l], alpha, B * L)
            dest = dest[B * L:]
        x = moe_finish(y_sorted[dest.T], tok_l, x_mid, *shared_w, g2, ln2_g[l], ln2_b[l], alpha, N)
    return x.reshape(B, N, D)
```
